```python
import jax, jax.numpy as jnp
from jax import lax
import numpy as np

D_MODEL = 1024
BATCH = 1
SEQ = 16384
DEPTH = 4

MLA_HEADS = 8
QK_NOPE_DIM = 128
QK_ROPE_DIM = 64
V_HEAD_DIM = 128
Q_LORA_RANK = 384
KV_LORA_RANK = 256
ROPE_THETA = 10000.0
Q_BLOCK = 128
FOURIER_GROUPS = 4
FOURIER_GROUP_DIM = 128
FOURIER_WIDTH = FOURIER_GROUPS * FOURIER_GROUP_DIM
EVEN_IN_WIDTH = Q_LORA_RANK + KV_LORA_RANK + QK_ROPE_DIM + FOURIER_WIDTH
EVEN_MIX_WIDTH = MLA_HEADS * V_HEAD_DIM + FOURIER_WIDTH
SGU_CHUNK = 128
SGU_GROUPS = 8
SGU_WIDTH = 2 * D_MODEL
SGU_GROUP_DIM = SGU_WIDTH // SGU_GROUPS
D_FF = ((8 * D_MODEL + 3 * 256 - 1) // (3 * 256)) * 256
N_EVEN = (DEPTH + 1) // 2
N_ODD = DEPTH // 2
DN_ALPHA = (2 * DEPTH) ** 0.25
DN_BETA = (8 * DEPTH) ** -0.25
LN_EPS = 1e-5
RMS_EPS = 1e-6

kernel_name = "hybrid_mla_fnet_sgu_deepnorm_encoder"


def layer_norm(x, g, b):
    xf = x.astype(jnp.float32)
    mu = jnp.mean(xf, axis=-1, keepdims=True)
    var = jnp.mean(jnp.square(xf - mu), axis=-1, keepdims=True)
    return ((xf - mu) * lax.rsqrt(var + LN_EPS) * g + b).astype(x.dtype)


def rms_norm(x, g):
    xf = x.astype(jnp.float32)
    ms = jnp.mean(jnp.square(xf), axis=-1, keepdims=True)
    return (xf * lax.rsqrt(ms + RMS_EPS) * g).astype(x.dtype)


def rotary_tables(seq):
    inv = 1.0 / (ROPE_THETA ** (jnp.arange(0, QK_ROPE_DIM, 2, dtype=jnp.float32) / QK_ROPE_DIM))
    ang = jnp.arange(seq, dtype=jnp.float32)[:, None] * inv[None, :]
    return jnp.cos(ang), jnp.sin(ang)


def apply_rotary(t, cos, sin):
    tf = t.astype(jnp.float32)
    t1, t2 = jnp.split(tf, 2, axis=-1)
    return jnp.concatenate([t1 * cos - t2 * sin, t1 * sin + t2 * cos], axis=-1).astype(t.dtype)


def mla_attention(q_nope, q_rope, k_nope, k_rope, v):
    B, S, H, _ = q_nope.shape
    nb = S // Q_BLOCK
    scale = (QK_NOPE_DIM + QK_ROPE_DIM) ** -0.5

    def to_blocks(t):
        return jnp.moveaxis(t.reshape(B, nb, Q_BLOCK, *t.shape[2:]), 1, 0)

    def attend(blk):
        qn, qr = blk
        s = (jnp.einsum('bqhd,bkhd->bhqk', qn, k_nope, preferred_element_type=jnp.float32)
             + jnp.einsum('bqhr,bkr->bhqk', qr, k_rope, preferred_element_type=jnp.float32))
        p = jax.nn.softmax(s * scale, axis=-1)
        return jnp.einsum('bhqk,bkhd->bqhd', p.astype(v.dtype), v)

    out = lax.map(attend, (to_blocks(q_nope), to_blocks(q_rope)))
    return jnp.moveaxis(out, 0, 1).reshape(B, S, H * V_HEAD_DIM)


def fourier_mix(f):
    B, S, _ = f.shape
    fg = f.astype(jnp.float32).reshape(B, S, FOURIER_GROUPS, FOURIER_GROUP_DIM)
    y = jnp.fft.fft2(fg, axes=(1, 3), norm="ortho").real
    return y.reshape(B, S, FOURIER_WIDTH).astype(f.dtype)


def even_mixer(x, w_in, q_norm, w_uq, kv_norm, w_uk, w_uv, w_out, cos, sin):
    B, S, _ = x.shape
    h = x @ w_in
    c_q, c_kv, k_r, f = jnp.split(
        h, [Q_LORA_RANK, Q_LORA_RANK + KV_LORA_RANK, Q_LORA_RANK + KV_LORA_RANK + QK_ROPE_DIM], axis=-1)
    q = (rms_norm(c_q, q_norm) @ w_uq).reshape(B, S, MLA_HEADS, QK_NOPE_DIM + QK_ROPE_DIM)
    q_nope = q[..., :QK_NOPE_DIM]
    q_rope = apply_rotary(q[..., QK_NOPE_DIM:], cos[:, None, :], sin[:, None, :])
    c_kv = rms_norm(c_kv, kv_norm)
    k_nope = (c_kv @ w_uk).reshape(B, S, MLA_HEADS, QK_NOPE_DIM)
    v = (c_kv @ w_uv).reshape(B, S, MLA_HEADS, V_HEAD_DIM)
    k_rope = apply_rotary(k_r, cos, sin)
    attn = mla_attention(q_nope, q_rope, k_nope, k_rope, v)
    return jnp.concatenate([attn, fourier_mix(f)], axis=-1) @ w_out


def odd_mixer(x, w_in, norm_g, norm_b, w_s, b_s, w_out):
    B, S, _ = x.shape
    z = jax.nn.gelu(x @ w_in, approximate=False)
    u, v = jnp.split(z, 2, axis=-1)
    v = layer_norm(v, norm_g, norm_b)
    vc = v.reshape(B, S // SGU_CHUNK, SGU_CHUNK, SGU_GROUPS, SGU_GROUP_DIM)
    s = (jnp.einsum('gpq,bnqgc->bnpgc', w_s, vc)
         + jnp.swapaxes(b_s, 0, 1)[None, None, :, :, None])
    return (u * s.reshape(B, S, SGU_WIDTH)) @ w_out


def swiglu(x, w_gate, w_up, w_down):
    return (jax.nn.silu(x @ w_gate) * (x @ w_up)) @ w_down


def setup_inputs(seed: int = 0) -> dict:
    key = jax.random.key(seed)
    ks = jax.random.split(key, 24)

    def nrm(k, shape, scale):
        return jax.random.normal(k, shape, jnp.float32) * scale

    def gain(k, shape):
        return 1.0 + 0.02 * jax.random.normal(k, shape, jnp.float32)

    def small(k, shape):
        return 0.02 * jax.random.normal(k, shape, jnp.float32)

    E, O, L = N_EVEN, N_ODD, DEPTH
    return {
        "x": jax.random.normal(ks[0], (BATCH, SEQ, D_MODEL), jnp.float32),
        "even_w_in": nrm(ks[1], (E, D_MODEL, EVEN_IN_WIDTH), D_MODEL ** -0.5),
        "even_q_norm": gain(ks[2], (E, Q_LORA_RANK)),
        "even_w_uq": nrm(ks[3], (E, Q_LORA_RANK, MLA_HEADS * (QK_NOPE_DIM + QK_ROPE_DIM)), Q_LORA_RANK ** -0.5),
        "even_kv_norm": gain(ks[4], (E, KV_LORA_RANK)),
        "even_w_uk": nrm(ks[5], (E, KV_LORA_RANK, MLA_HEADS * QK_NOPE_DIM), KV_LORA_RANK ** -0.5),
        "even_w_uv": nrm(ks[6], (E, KV_LORA_RANK, MLA_HEADS * V_HEAD_DIM), KV_LORA_RANK ** -0.5 * DN_BETA),
        "even_w_out": nrm(ks[7], (E, EVEN_MIX_WIDTH, D_MODEL), EVEN_MIX_WIDTH ** -0.5 * DN_BETA),
        "odd_w_in": nrm(ks[8], (O, D_MODEL, 2 * SGU_WIDTH), D_MODEL ** -0.5),
        "odd_sgu_norm_g": gain(ks[9], (O, SGU_WIDTH)),
        "odd_sgu_norm_b": small(ks[10], (O, SGU_WIDTH)),
        "odd_w_spatial": nrm(ks[11], (O, SGU_GROUPS, SGU_CHUNK, SGU_CHUNK), SGU_CHUNK ** -0.5),
        "odd_b_spatial": gain(ks[12], (O, SGU_GROUPS, SGU_CHUNK)),
        "odd_w_out": nrm(ks[13], (O, SGU_WIDTH, D_MODEL), SGU_WIDTH ** -0.5 * DN_BETA),
        "mix_ln_g": gain(ks[14], (L, D_MODEL)),
        "mix_ln_b": small(ks[15], (L, D_MODEL)),
        "ffn_w_gate": nrm(ks[16], (L, D_MODEL, D_FF), D_MODEL ** -0.5),
        "ffn_w_up": nrm(ks[17], (L, D_MODEL, D_FF), D_MODEL ** -0.5 * DN_BETA),
        "ffn_w_down": nrm(ks[18], (L, D_FF, D_MODEL), D_FF ** -0.5 * DN_BETA),
        "ffn_ln_g": gain(ks[19], (L, D_MODEL)),
        "ffn_ln_b": small(ks[20], (L, D_MODEL)),
    }


def reference(x, even_w_in, even_q_norm, even_w_uq, even_kv_norm, even_w_uk, even_w_uv, even_w_out,
              odd_w_in, odd_sgu_norm_g, odd_sgu_norm_b, odd_w_spatial, odd_b_spatial, odd_w_out,
              mix_ln_g, mix_ln_b, ffn_w_gate, ffn_w_up, ffn_w_down, ffn_ln_g, ffn_ln_b):
    cos, sin = rotary_tables(x.shape[1])
    for layer in range(DEPTH):
        i = layer // 2
        if layer % 2 == 0:
            y = even_mixer(x, even_w_in[i], even_q_norm[i], even_w_uq[i], even_kv_norm[i],
                           even_w_uk[i], even_w_uv[i], even_w_out[i], cos, sin)
        else:
            y = odd_mixer(x, odd_w_in[i], odd_sgu_norm_g[i], odd_sgu_norm_b[i],
                          odd_w_spatial[i], odd_b_spatial[i], odd_w_out[i])
        x = layer_norm(DN_ALPHA * x + y, mix_ln_g[layer], mix_ln_b[layer])
        x = layer_norm(DN_ALPHA * x + swiglu(x, ffn_w_gate[layer], ffn_w_up[layer], ffn_w_down[layer]),
                       ffn_ln_g[layer], ffn_ln_b[layer])
    return x
```

```python
import functools
import math

import numpy as np
import jax
import jax.numpy as jnp
from jax import lax
from jax.experimental import pallas as pl
from jax.experimental.pallas import tpu as pltpu

F32 = jnp.float32
BF16 = jnp.bfloat16

D_MODEL = 1024
SEQ = 16384
DEPTH = 4
HEADS = 8
NOPE = 128
ROPE = 64
QK_DIM = NOPE + ROPE
V_DIM = 128
Q_RANK = 384
KV_RANK = 256
ROPE_THETA = 10000.0
FGROUPS = 4
FDIM = 128
FWIDTH = FGROUPS * FDIM
SGU_CHUNK = 128
SGU_GROUPS = 8
SGU_WIDTH = 2 * D_MODEL
SGU_GDIM = SGU_WIDTH // SGU_GROUPS
D_FF = 2816
DN_ALPHA = (2 * DEPTH) ** 0.25
LN_EPS = 1e-5
RMS_EPS = 1e-6

VMEM_LIMIT_BYTES = 56 * 1024 * 1024
DFT_N = 128

ROW_TILE = 512
KV_TILE = 512
Q_TILE = 256
FF_CHUNK = 1408
SGU_ROWS = 256

_HI = lax.Precision.HIGHEST
_NT = (((1,), (1,)), ((), ()))


def _params(n_axes):
    return pltpu.CompilerParams(
        dimension_semantics=("arbitrary",) * n_axes, vmem_limit_bytes=VMEM_LIMIT_BYTES)


def _resident(shape):
    nd = len(shape)
    return pl.BlockSpec(shape, lambda *_: (0,) * nd, pipeline_mode=pl.Buffered(1))


def _layer_norm(z, g, b):
    mu = jnp.mean(z, axis=-1, keepdims=True)
    zc = z - mu
    var = jnp.mean(zc * zc, axis=-1, keepdims=True)
    return zc * lax.rsqrt(var + LN_EPS) * g + b


def _rms_norm(z, g):
    ms = jnp.mean(z * z, axis=-1, keepdims=True)
    return z * lax.rsqrt(ms + RMS_EPS) * g


def _even_in_kernel(x_ref, win_ref, qn_ref, wuqT_ref, kvn_ref, wuk_ref, wuvT_ref,
                    cosT_ref, sinT_ref, cc_ref, ss_ref,
                    qT_ref, k_ref, vT_ref, f_ref, *, q_scale):
    tm = x_ref.shape[0]
    n_kv = tm // KV_TILE
    xb = x_ref[...].astype(BF16)
    h = jnp.dot(xb, win_ref[...], preferred_element_type=F32)
    cq = _rms_norm(h[:, :Q_RANK], qn_ref[...]).astype(BF16)
    ckv = _rms_norm(h[:, Q_RANK:Q_RANK + KV_RANK], kvn_ref[...]).astype(BF16)
    f_ref[...] = h[:, Q_RANK + KV_RANK:Q_RANK + KV_RANK + FWIDTH]
    kr = h[:, Q_RANK + KV_RANK + FWIDTH:]
    kr_rot = kr * cc_ref[...] + pltpu.roll(kr, ROPE, axis=1) * ss_ref[...]
    kr_rot = kr_rot[:, :ROPE].astype(BF16)

    qT = lax.dot_general(wuqT_ref[...], cq, _NT, preferred_element_type=F32)
    cosT = cosT_ref[...]
    sinT = sinT_ref[...]
    half = ROPE // 2
    for hd in range(HEADS):
        base = hd * QK_DIM
        qT_ref[hd, 0:NOPE, :] = (qT[base:base + NOPE] * q_scale).astype(BF16)
        t1 = qT[base + NOPE:base + NOPE + half]
        t2 = qT[base + NOPE + half:base + QK_DIM]
        qT_ref[hd, NOPE:NOPE + half, :] = ((t1 * cosT - t2 * sinT) * q_scale).astype(BF16)
        qT_ref[hd, NOPE + half:QK_DIM, :] = ((t1 * sinT + t2 * cosT) * q_scale).astype(BF16)

    kn = jnp.dot(ckv, wuk_ref[...], preferred_element_type=F32)
    vT = lax.dot_general(wuvT_ref[...], ckv, _NT, preferred_element_type=F32)
    for hd in range(HEADS):
        for c in range(n_kv):
            rows = slice(c * KV_TILE, (c + 1) * KV_TILE)
            k_ref[hd, c, :, 0:NOPE] = kn[rows, hd * NOPE:(hd + 1) * NOPE].astype(BF16)
            k_ref[hd, c, :, NOPE:QK_DIM] = kr_rot[rows]
            vT_ref[hd, c] = vT[hd * V_DIM:(hd + 1) * V_DIM, rows].astype(BF16)


def _even_in(x, win, qn, wuqT, kvn, wuk, wuvT, cosT, sinT, cc, ss):
    S = x.shape[0]
    tm = ROW_TILE
    n_kv = tm // KV_TILE
    win_w = win.shape[1]
    q_scale = (QK_DIM ** -0.5) * math.log2(math.e)
    row = lambda i: (i, 0)
    return pl.pallas_call(
        functools.partial(_even_in_kernel, q_scale=q_scale),
        grid=(S // tm,),
        in_specs=[
            pl.BlockSpec((tm, D_MODEL), row),
            _resident((D_MODEL, win_w)),
            _resident((1, Q_RANK)),
            _resident((HEADS * QK_DIM, Q_RANK)),
            _resident((1, KV_RANK)),
            _resident((KV_RANK, HEADS * NOPE)),
            _resident((HEADS * V_DIM, KV_RANK)),
            pl.BlockSpec((ROPE // 2, tm), lambda i: (0, i)),
            pl.BlockSpec((ROPE // 2, tm), lambda i: (0, i)),
            pl.BlockSpec((tm, 2 * ROPE), row),
            pl.BlockSpec((tm, 2 * ROPE), row),
        ],
        out_specs=[
            pl.BlockSpec((HEADS, QK_DIM, tm), lambda i: (0, 0, i)),
            pl.BlockSpec((HEADS, n_kv, KV_TILE, QK_DIM), lambda i: (0, i, 0, 0)),
            pl.BlockSpec((HEADS, n_kv, V_DIM, KV_TILE), lambda i: (0, i, 0, 0)),
            pl.BlockSpec((tm, FWIDTH), row),
        ],
        out_shape=[
            jax.ShapeDtypeStruct((HEADS, QK_DIM, S), BF16),
            jax.ShapeDtypeStruct((HEADS, S // KV_TILE, KV_TILE, QK_DIM), BF16),
            jax.ShapeDtypeStruct((HEADS, S // KV_TILE, V_DIM, KV_TILE), BF16),
            jax.ShapeDtypeStruct((S, FWIDTH), F32),
        ],
        compiler_params=_params(1),
        name="even_in",
    )(x, win, qn, wuqT, kvn, wuk, wuvT, cosT, sinT, cc, ss)


def _attn_kernel(qT_ref, k_ref, vT_ref, o_ref):
    n_kv = k_ref.shape[0]
    tq = qT_ref.shape[1]
    qT = qT_ref[...]

    def step(j, carry):
        m, l, acc = carry
        s = jnp.dot(k_ref[j], qT, preferred_element_type=F32)
        m_new = jnp.maximum(m, jnp.max(s, axis=0, keepdims=True))
        alpha = jnp.exp2(m - m_new)
        p = jnp.exp2(s - m_new)
        l = alpha * l + jnp.sum(p, axis=0, keepdims=True)
        pv = jnp.dot(vT_ref[j], p.astype(BF16), preferred_element_type=F32)
        return m_new, l, alpha * acc + pv

    m0 = jnp.full((1, tq), -jnp.inf, F32)
    l0 = jnp.zeros((1, tq), F32)
    acc0 = jnp.zeros((V_DIM, tq), F32)
    _, l, acc = lax.fori_loop(0, n_kv, step, (m0, l0, acc0))
    o_ref[...] = (acc / l).T.astype(o_ref.dtype)


def _attention(qT, k, vT):
    H, n_kv, tk, _ = k.shape
    S = qT.shape[2]
    tq = Q_TILE
    return pl.pallas_call(
        _attn_kernel,
        grid=(H, S // tq),
        in_specs=[
            pl.BlockSpec((None, QK_DIM, tq), lambda h, i: (h, 0, i)),
            pl.BlockSpec((None, n_kv, tk, QK_DIM), lambda h, i: (h, 0, 0, 0)),
            pl.BlockSpec((None, n_kv, V_DIM, tk), lambda h, i: (h, 0, 0, 0)),
        ],
        out_specs=pl.BlockSpec((tq, V_DIM), lambda h, i: (i, h)),
        out_shape=jax.ShapeDtypeStruct((S, H * V_DIM), BF16),
        compiler_params=_params(2),
        name="mla_attention",
    )(qT, k, vT)


def _dft_consts():
    n = np.arange(DFT_N)
    ang = 2.0 * np.pi * np.outer(n, n) / DFT_N
    c = np.cos(ang) / np.sqrt(DFT_N)
    s = np.sin(ang) / np.sqrt(DFT_N)
    w_a = np.concatenate([c, -s], axis=0)
    w_b = np.block([[c, s], [-s, c]])
    w_c = np.concatenate([c, s], axis=0)
    return (jnp.asarray(w_a, F32), jnp.asarray(w_b, F32), jnp.asarray(w_c, F32))


def _dft_a_kernel(x_ref, w_ref, tr_ref, ti_ref):
    t = jnp.dot(w_ref[...], x_ref[...], preferred_element_type=F32, precision=_HI)
    tr_ref[...] = t[:DFT_N]
    ti_ref[...] = t[DFT_N:]


def _dft_b_kernel(tr_ref, ti_ref, twc_ref, tws_ref, wb_ref, wc_ref, y_ref):
    n_blk = tr_ref.shape[0] // DFT_N
    wb = wb_ref[...]
    wc = wc_ref[...]
    for j in range(n_blk):
        rows = slice(j * DFT_N, (j + 1) * DFT_N)
        tr = tr_ref[rows, :]
        ti = ti_ref[rows, :]
        cw = jnp.concatenate([twc_ref[rows, :]] * FGROUPS, axis=1)
        sw = jnp.concatenate([tws_ref[rows, :]] * FGROUPS, axis=1)
        st = jnp.concatenate([tr * cw + ti * sw, ti * cw - tr * sw], axis=0)
        g = jnp.dot(wb, st, preferred_element_type=F32, precision=_HI)
        for grp in range(FGROUPS):
            cols = slice(grp * FDIM, (grp + 1) * FDIM)
            gg = jnp.concatenate([g[:DFT_N, cols], g[DFT_N:, cols]], axis=1)
            y_ref[rows, cols] = jnp.dot(gg, wc, preferred_element_type=F32, precision=_HI)


def _fourier_mix(f, twc, tws):
    S = f.shape[0]
    w_a, w_b, w_c = _dft_consts()
    x2 = f.reshape(DFT_N, DFT_N * FWIDTH)
    tn = 4096
    tr, ti = pl.pallas_call(
        _dft_a_kernel,
        grid=(x2.shape[1] // tn,),
        in_specs=[pl.BlockSpec((DFT_N, tn), lambda i: (0, i)), _resident(w_a.shape)],
        out_specs=[pl.BlockSpec((DFT_N, tn), lambda i: (0, i))] * 2,
        out_shape=[jax.ShapeDtypeStruct(x2.shape, F32)] * 2,
        compiler_params=_params(1),
        name="dft_positions_outer",
    )(x2, w_a)
    tr = tr.reshape(S, FWIDTH)
    ti = ti.reshape(S, FWIDTH)
    tb = 8 * DFT_N
    blk = lambda i: (i, 0)
    yt = pl.pallas_call(
        _dft_b_kernel,
        grid=(S // tb,),
        in_specs=[pl.BlockSpec((tb, FWIDTH), blk), pl.BlockSpec((tb, FWIDTH), blk),
                  pl.BlockSpec((tb, FDIM), blk), pl.BlockSpec((tb, FDIM), blk),
                  _resident(w_b.shape), _resident(w_c.shape)],
        out_specs=pl.BlockSpec((tb, FWIDTH), blk),
        out_shape=jax.ShapeDtypeStruct((S, FWIDTH), F32),
        compiler_params=_params(1),
        name="dft_positions_inner_channels",
    )(tr, ti, twc, tws, w_b, w_c)
    return yt.reshape(DFT_N, DFT_N, FWIDTH).transpose(1, 0, 2).reshape(S, FWIDTH)


def _even_out_kernel(a_ref, f_ref, x_ref, wa_ref, wf_ref, g_ref, b_ref, o_ref):
    y = jnp.dot(a_ref[...], wa_ref[...], preferred_element_type=F32)
    y = y + jnp.dot(f_ref[...].astype(BF16), wf_ref[...], preferred_element_type=F32)
    o_ref[...] = _layer_norm(DN_ALPHA * x_ref[...] + y, g_ref[...], b_ref[...])


def _even_out(attn, four, x, wa, wf, g, b):
    S = x.shape[0]
    tm = ROW_TILE
    row = lambda i: (i, 0)
    return pl.pallas_call(
        _even_out_kernel,
        grid=(S // tm,),
        in_specs=[pl.BlockSpec((tm, HEADS * V_DIM), row), pl.BlockSpec((tm, FWIDTH), row),
                  pl.BlockSpec((tm, D_MODEL), row),
                  _resident(wa.shape), _resident(wf.shape),
                  _resident((1, D_MODEL)), _resident((1, D_MODEL))],
        out_specs=pl.BlockSpec((tm, D_MODEL), row),
        out_shape=jax.ShapeDtypeStruct((S, D_MODEL), F32),
        compiler_params=_params(1),
        name="even_out_ln",
    )(attn, four, x, wa, wf, g, b)


def _odd_kernel(x_ref, win_ref, ng_ref, nb_ref, ws_ref, bs_ref, wout_ref, g_ref, b_ref, o_ref):
    tm = x_ref.shape[0]
    x = x_ref[...]
    z = jnp.dot(x.astype(BF16), win_ref[...], preferred_element_type=F32)
    z = 0.5 * z * (1.0 + lax.erf(z * (2.0 ** -0.5)))
    u = z[:, :SGU_WIDTH]
    v = _layer_norm(z[:, SGU_WIDTH:], ng_ref[...], nb_ref[...]).astype(BF16)
    gated = []
    for c in range(tm // SGU_CHUNK):
        rows = slice(c * SGU_CHUNK, (c + 1) * SGU_CHUNK)
        parts = []
        for grp in range(SGU_GROUPS):
            cols = slice(grp * SGU_GDIM, (grp + 1) * SGU_GDIM)
            s = jnp.dot(ws_ref[grp], v[rows, cols], preferred_element_type=F32) + bs_ref[grp]
            parts.append((u[rows, cols] * s).astype(BF16))
        gated.append(jnp.concatenate(parts, axis=1))
    gated = jnp.concatenate(gated, axis=0)
    y = jnp.dot(gated, wout_ref[...], preferred_element_type=F32)
    o_ref[...] = _layer_norm(DN_ALPHA * x + y, g_ref[...], b_ref[...])


def _odd_layer(x, win, ng, nb, ws, bs, wout, g, b):
    S = x.shape[0]
    tm = SGU_ROWS
    row = lambda i: (i, 0)
    return pl.pallas_call(
        _odd_kernel,
        grid=(S // tm,),
        in_specs=[pl.BlockSpec((tm, D_MODEL), row),
                  _resident(win.shape), _resident(ng.shape), _resident(nb.shape),
                  _resident(ws.shape), _resident(bs.shape), _resident(wout.shape),
                  _resident((1, D_MODEL)), _resident((1, D_MODEL))],
        out_specs=pl.BlockSpec((tm, D_MODEL), row),
        out_shape=jax.ShapeDtypeStruct((S, D_MODEL), F32),
        compiler_params=_params(1),
        name="odd_sgu_ln",
    )(x, win, ng, nb, ws, bs, wout, g, b)


def _ffn_kernel(x_ref, wg_ref, wu_ref, wd_ref, g_ref, b_ref, o_ref):
    x = x_ref[...]
    xb = x.astype(BF16)
    y = None
    for c in range(D_FF // FF_CHUNK):
        cols = slice(c * FF_CHUNK, (c + 1) * FF_CHUNK)
        gate = jnp.dot(xb, wg_ref[:, cols], preferred_element_type=F32)
        up = jnp.dot(xb, wu_ref[:, cols], preferred_element_type=F32)
        hid = (jax.nn.silu(gate) * up).astype(BF16)
        part = jnp.dot(hid, wd_ref[cols, :], preferred_element_type=F32)
        y = part if y is None else y + part
    o_ref[...] = _layer_norm(DN_ALPHA * x + y, g_ref[...], b_ref[...])


def _ffn(x, wg, wu, wd, g, b):
    S = x.shape[0]
    tm = ROW_TILE
    row = lambda i: (i, 0)
    return pl.pallas_call(
        _ffn_kernel,
        grid=(S // tm,),
        in_specs=[pl.BlockSpec((tm, D_MODEL), row),
                  _resident(wg.shape), _resident(wu.shape), _resident(wd.shape),
                  _resident((1, D_MODEL)), _resident((1, D_MODEL))],
        out_specs=pl.BlockSpec((tm, D_MODEL), row),
        out_shape=jax.ShapeDtypeStruct((S, D_MODEL), F32),
        compiler_params=_params(1),
        name="swiglu_ln",
    )(x, wg, wu, wd, g, b)


def _rotary_tables(seq):
    inv = 1.0 / (ROPE_THETA ** (jnp.arange(0, ROPE, 2, dtype=F32) / ROPE))
    ang = jnp.arange(seq, dtype=F32)[:, None] * inv[None, :]
    cos, sin = jnp.cos(ang), jnp.sin(ang)
    cc = jnp.concatenate([cos, cos, cos, cos], axis=1)
    ss = jnp.concatenate([-sin, sin, -sin, sin], axis=1)
    return cos.T, sin.T, cc, ss


def _twiddle_tables():
    k2 = jnp.arange(DFT_N, dtype=jnp.int32)[:, None]
    n1 = jnp.arange(DFT_N, dtype=jnp.int32)[None, :]
    ang = (k2 * n1).astype(F32) * F32(2.0 * np.pi / SEQ)
    ang = ang.reshape(SEQ, 1)
    return (jnp.broadcast_to(jnp.cos(ang), (SEQ, FDIM)),
            jnp.broadcast_to(jnp.sin(ang), (SEQ, FDIM)))


def kernel(x, even_w_in, even_q_norm, even_w_uq, even_kv_norm, even_w_uk, even_w_uv, even_w_out,
           odd_w_in, odd_sgu_norm_g, odd_sgu_norm_b, odd_w_spatial, odd_b_spatial, odd_w_out,
           mix_ln_g, mix_ln_b, ffn_w_gate, ffn_w_up, ffn_w_down, ffn_ln_g, ffn_ln_b):
    B, S, D = x.shape
    assert (B, S, D) == (1, SEQ, D_MODEL)
    cosT, sinT, cc, ss = _rotary_tables(S)
    twc, tws = _twiddle_tables()
    h = x.reshape(S, D)
    half = ROPE // 2
    c0 = Q_RANK + KV_RANK
    c1 = c0 + ROPE
    for layer in range(DEPTH):
        i = layer // 2
        if layer % 2 == 0:
            w = even_w_in[i]
            win = jnp.concatenate(
                [w[:, :c0], w[:, c1:], w[:, c0:c1], w[:, c0 + half:c1], w[:, c0:c0 + half]],
                axis=1).astype(BF16)
            qT, k, vT, f = _even_in(
                h, win, even_q_norm[i].reshape(1, -1), even_w_uq[i].T.astype(BF16),
                even_kv_norm[i].reshape(1, -1), even_w_uk[i].astype(BF16),
                even_w_uv[i].T.astype(BF16), cosT, sinT, cc, ss)
            attn = _attention(qT, k, vT)
            four = _fourier_mix(f, twc, tws)
            wo = even_w_out[i].astype(BF16)
            h = _even_out(attn, four, h, wo[:HEADS * V_DIM], wo[HEADS * V_DIM:],
                          mix_ln_g[layer].reshape(1, -1), mix_ln_b[layer].reshape(1, -1))
        else:
            h = _odd_layer(h, odd_w_in[i].astype(BF16),
                           odd_sgu_norm_g[i].reshape(1, -1), odd_sgu_norm_b[i].reshape(1, -1),
                           odd_w_spatial[i].astype(BF16),
                           odd_b_spatial[i].reshape(SGU_GROUPS, SGU_CHUNK, 1),
                           odd_w_out[i].astype(BF16),
                           mix_ln_g[layer].reshape(1, -1), mix_ln_b[layer].reshape(1, -1))
        h = _ffn(h, ffn_w_gate[layer].astype(BF16), ffn_w_up[layer].astype(BF16),
                 ffn_w_down[layer].astype(BF16),
                 ffn_ln_g[layer].reshape(1, -1), ffn_ln_b[layer].reshape(1, -1))
    return h.reshape(B, S, D)
```

```python
import functools
import math

import numpy as np
import jax
import jax.numpy as jnp
from jax import lax
from jax.experimental import pallas as pl
from jax.experimental.pallas import tpu as pltpu

F32 = jnp.float32
BF16 = jnp.bfloat16

D_MODEL = 1024
SEQ = 16384
DEPTH = 4
HEADS = 8
NOPE = 128
ROPE = 64
QK_DIM = NOPE + ROPE
V_DIM = 128
Q_RANK = 384
KV_RANK = 256
ROPE_THETA = 10000.0
FGROUPS = 4
FDIM = 128
FWIDTH = FGROUPS * FDIM
SGU_CHUNK = 128
SGU_GROUPS = 8
SGU_WIDTH = 2 * D_MODEL
SGU_GDIM = SGU_WIDTH // SGU_GROUPS
D_FF = 2816
DN_ALPHA = (2 * DEPTH) ** 0.25
LN_EPS = 1e-5
RMS_EPS = 1e-6

VMEM_LIMIT_BYTES = 56 * 1024 * 1024
DFT_N = 128

ROW_TILE = 512
KV_TILE = 512
Q_TILE = 1024
FF_CHUNK = 1408
SGU_ROWS = 256

_HI = lax.Precision.HIGHEST
_NT = (((1,), (1,)), ((), ()))


def _params(n_axes):
    return pltpu.CompilerParams(
        dimension_semantics=("arbitrary",) * n_axes, vmem_limit_bytes=VMEM_LIMIT_BYTES)


def _resident(shape):
    nd = len(shape)
    return pl.BlockSpec(shape, lambda *_: (0,) * nd, pipeline_mode=pl.Buffered(1))


def _layer_norm(z, g, b):
    mu = jnp.mean(z, axis=-1, keepdims=True)
    zc = z - mu
    var = jnp.mean(zc * zc, axis=-1, keepdims=True)
    return zc * lax.rsqrt(var + LN_EPS) * g + b


def _rms_norm(z, g):
    ms = jnp.mean(z * z, axis=-1, keepdims=True)
    return z * lax.rsqrt(ms + RMS_EPS) * g


def _even_in_kernel(x_ref, win_ref, qn_ref, wuqT_ref, kvn_ref, wuk_ref, wuvT_ref,
                    cosT_ref, sinT_ref, cc_ref, ss_ref,
                    qT_ref, k_ref, vT_ref, f_ref, *, q_scale):
    tm = x_ref.shape[0]
    n_kv = tm // KV_TILE
    xb = x_ref[...].astype(BF16)
    h = jnp.dot(xb, win_ref[...], preferred_element_type=F32)
    cq = _rms_norm(h[:, :Q_RANK], qn_ref[...]).astype(BF16)
    ckv = _rms_norm(h[:, Q_RANK:Q_RANK + KV_RANK], kvn_ref[...]).astype(BF16)
    f_ref[...] = h[:, Q_RANK + KV_RANK:Q_RANK + KV_RANK + FWIDTH]
    kr = h[:, Q_RANK + KV_RANK + FWIDTH:]
    kr_rot = kr * cc_ref[...] + pltpu.roll(kr, ROPE, axis=1) * ss_ref[...]
    kr_rot = kr_rot[:, :ROPE].astype(BF16)

    qT = lax.dot_general(wuqT_ref[...], cq, _NT, preferred_element_type=F32)
    cosT = cosT_ref[...]
    sinT = sinT_ref[...]
    half = ROPE // 2
    for hd in range(HEADS):
        base = hd * QK_DIM
        qT_ref[hd, 0:NOPE, :] = (qT[base:base + NOPE] * q_scale).astype(BF16)
        t1 = qT[base + NOPE:base + NOPE + half]
        t2 = qT[base + NOPE + half:base + QK_DIM]
        qT_ref[hd, NOPE:NOPE + half, :] = ((t1 * cosT - t2 * sinT) * q_scale).astype(BF16)
        qT_ref[hd, NOPE + half:QK_DIM, :] = ((t1 * sinT + t2 * cosT) * q_scale).astype(BF16)

    kn = jnp.dot(ckv, wuk_ref[...], preferred_element_type=F32)
    vT = lax.dot_general(wuvT_ref[...], ckv, _NT, preferred_element_type=F32)
    for hd in range(HEADS):
        for c in range(n_kv):
            rows = slice(c * KV_TILE, (c + 1) * KV_TILE)
            k_ref[hd, c, :, 0:NOPE] = kn[rows, hd * NOPE:(hd + 1) * NOPE].astype(BF16)
            k_ref[hd, c, :, NOPE:QK_DIM] = kr_rot[rows]
            vT_ref[hd, c] = vT[hd * V_DIM:(hd + 1) * V_DIM, rows].astype(BF16)


def _even_in(x, win, qn, wuqT, kvn, wuk, wuvT, cosT, sinT, cc, ss):
    S = x.shape[0]
    tm = ROW_TILE
    n_kv = tm // KV_TILE
    win_w = win.shape[1]
    q_scale = (QK_DIM ** -0.5) * math.log2(math.e)
    row = lambda i: (i, 0)
    return pl.pallas_call(
        functools.partial(_even_in_kernel, q_scale=q_scale),
        grid=(S // tm,),
        in_specs=[
            pl.BlockSpec((tm, D_MODEL), row),
            _resident((D_MODEL, win_w)),
            _resident((1, Q_RANK)),
            _resident((HEADS * QK_DIM, Q_RANK)),
            _resident((1, KV_RANK)),
            _resident((KV_RANK, HEADS * NOPE)),
            _resident((HEADS * V_DIM, KV_RANK)),
            pl.BlockSpec((ROPE // 2, tm), lambda i: (0, i)),
            pl.BlockSpec((ROPE // 2, tm), lambda i: (0, i)),
            pl.BlockSpec((tm, 2 * ROPE), row),
            pl.BlockSpec((tm, 2 * ROPE), row),
        ],
        out_specs=[
            pl.BlockSpec((HEADS, QK_DIM, tm), lambda i: (0, 0, i)),
            pl.BlockSpec((HEADS, n_kv, KV_TILE, QK_DIM), lambda i: (0, i, 0, 0)),
            pl.BlockSpec((HEADS, n_kv, V_DIM, KV_TILE), lambda i: (0, i, 0, 0)),
            pl.BlockSpec((tm, FWIDTH), row),
        ],
        out_shape=[
            jax.ShapeDtypeStruct((HEADS, QK_DIM, S), BF16),
            jax.ShapeDtypeStruct((HEADS, S // KV_TILE, KV_TILE, QK_DIM), BF16),
            jax.ShapeDtypeStruct((HEADS, S // KV_TILE, V_DIM, KV_TILE), BF16),
            jax.ShapeDtypeStruct((S, FWIDTH), F32),
        ],
        compiler_params=_params(1),
        name="even_in",
    )(x, win, qn, wuqT, kvn, wuk, wuvT, cosT, sinT, cc, ss)


def _attn_kernel(qT_ref, k_ref, vT_ref, o_ref, s0_ref, s1_ref, p0_ref, p1_ref, acc_ref):
    n_kv = k_ref.shape[0]
    tq = qT_ref.shape[1]
    qT = qT_ref[...]

    def scores(j, s_ref):
        s = jnp.dot(k_ref[j], qT, preferred_element_type=F32)
        s_ref[...] = s
        return jnp.max(s, axis=0, keepdims=True)

    def softmax(s_ref, p_ref, mx, m, l):
        m_new = jnp.maximum(m, mx)
        alpha = jnp.exp2(m - m_new)
        p = jnp.exp2(s_ref[...] - m_new)
        p_ref[...] = p.astype(BF16)
        return m_new, alpha * l + jnp.sum(p, axis=0, keepdims=True), alpha

    def values(j, p_ref, alpha):
        pv = jnp.dot(vT_ref[j], p_ref[...], preferred_element_type=F32)
        acc_ref[...] = alpha * acc_ref[...] + pv

    p1_ref[...] = jnp.zeros_like(p1_ref)
    acc_ref[...] = jnp.zeros_like(acc_ref)
    mx0 = scores(0, s0_ref)

    def pair(i, carry):
        mx0, m, l, alpha_prev = carry
        j = 2 * i
        mx1 = scores(j + 1, s1_ref)
        m, l, alpha0 = softmax(s0_ref, p0_ref, mx0, m, l)
        values(jnp.maximum(j - 1, 0), p1_ref, alpha_prev)
        mx0 = scores(jnp.minimum(j + 2, n_kv - 1), s0_ref)
        m, l, alpha1 = softmax(s1_ref, p1_ref, mx1, m, l)
        values(j, p0_ref, alpha0)
        return mx0, m, l, alpha1

    m0 = jnp.full((1, tq), -jnp.inf, F32)
    l0 = jnp.zeros((1, tq), F32)
    _, _, l, alpha = lax.fori_loop(0, n_kv // 2, pair, (mx0, m0, l0, jnp.ones((1, tq), F32)))
    values(n_kv - 1, p1_ref, alpha)
    o_ref[...] = (acc_ref[...] / l).T.astype(o_ref.dtype)


def _attention(qT, k, vT):
    H, n_kv, tk, _ = k.shape
    S = qT.shape[2]
    tq = Q_TILE
    assert n_kv % 2 == 0
    return pl.pallas_call(
        _attn_kernel,
        grid=(H, S // tq),
        in_specs=[
            pl.BlockSpec((None, QK_DIM, tq), lambda h, i: (h, 0, i)),
            pl.BlockSpec((None, n_kv, tk, QK_DIM), lambda h, i: (h, 0, 0, 0)),
            pl.BlockSpec((None, n_kv, V_DIM, tk), lambda h, i: (h, 0, 0, 0)),
        ],
        out_specs=pl.BlockSpec((tq, V_DIM), lambda h, i: (i, h)),
        out_shape=jax.ShapeDtypeStruct((S, H * V_DIM), BF16),
        scratch_shapes=[pltpu.VMEM((tk, tq), F32), pltpu.VMEM((tk, tq), F32),
                        pltpu.VMEM((tk, tq), BF16), pltpu.VMEM((tk, tq), BF16),
                        pltpu.VMEM((V_DIM, tq), F32)],
        compiler_params=_params(2),
        name="mla_attention",
    )(qT, k, vT)


def _dft_consts():
    n = np.arange(DFT_N)
    ang = 2.0 * np.pi * np.outer(n, n) / DFT_N
    c = np.cos(ang) / np.sqrt(DFT_N)
    s = np.sin(ang) / np.sqrt(DFT_N)
    w_a = np.concatenate([c, -s], axis=0)
    w_b = np.block([[c, s], [-s, c]])
    w_c = np.concatenate([c, s], axis=0)
    return (jnp.asarray(w_a, F32), jnp.asarray(w_b, F32), jnp.asarray(w_c, F32))


def _dft_a_kernel(x_ref, w_ref, tr_ref, ti_ref):
    t = jnp.dot(w_ref[...], x_ref[...], preferred_element_type=F32, precision=_HI)
    tr_ref[...] = t[:DFT_N]
    ti_ref[...] = t[DFT_N:]


def _dft_b_kernel(tr_ref, ti_ref, twc_ref, tws_ref, wb_ref, wc_ref, y_ref):
    n_blk = tr_ref.shape[0] // DFT_N
    wb = wb_ref[...]
    wc = wc_ref[...]
    for j in range(n_blk):
        rows = slice(j * DFT_N, (j + 1) * DFT_N)
        tr = tr_ref[rows, :]
        ti = ti_ref[rows, :]
        cw = jnp.concatenate([twc_ref[rows, :]] * FGROUPS, axis=1)
        sw = jnp.concatenate([tws_ref[rows, :]] * FGROUPS, axis=1)
        st = jnp.concatenate([tr * cw + ti * sw, ti * cw - tr * sw], axis=0)
        g = jnp.dot(wb, st, preferred_element_type=F32, precision=_HI)
        for grp in range(FGROUPS):
            cols = slice(grp * FDIM, (grp + 1) * FDIM)
            gg = jnp.concatenate([g[:DFT_N, cols], g[DFT_N:, cols]], axis=1)
            y_ref[rows, cols] = jnp.dot(gg, wc, preferred_element_type=F32, precision=_HI)


def _fourier_mix(f, twc, tws):
    S = f.shape[0]
    w_a, w_b, w_c = _dft_consts()
    x2 = f.reshape(DFT_N, DFT_N * FWIDTH)
    tn = 4096
    tr, ti = pl.pallas_call(
        _dft_a_kernel,
        grid=(x2.shape[1] // tn,),
        in_specs=[pl.BlockSpec((DFT_N, tn), lambda i: (0, i)), _resident(w_a.shape)],
        out_specs=[pl.BlockSpec((DFT_N, tn), lambda i: (0, i))] * 2,
        out_shape=[jax.ShapeDtypeStruct(x2.shape, F32)] * 2,
        compiler_params=_params(1),
        name="dft_positions_outer",
    )(x2, w_a)
    tr = tr.reshape(S, FWIDTH)
    ti = ti.reshape(S, FWIDTH)
    tb = 8 * DFT_N
    blk = lambda i: (i, 0)
    yt = pl.pallas_call(
        _dft_b_kernel,
        grid=(S // tb,),
        in_specs=[pl.BlockSpec((tb, FWIDTH), blk), pl.BlockSpec((tb, FWIDTH), blk),
                  pl.BlockSpec((tb, FDIM), blk), pl.BlockSpec((tb, FDIM), blk),
                  _resident(w_b.shape), _resident(w_c.shape)],
        out_specs=pl.BlockSpec((tb, FWIDTH), blk),
        out_shape=jax.ShapeDtypeStruct((S, FWIDTH), F32),
        compiler_params=_params(1),
        name="dft_positions_inner_channels",
    )(tr, ti, twc, tws, w_b, w_c)
    return yt.reshape(DFT_N, DFT_N, FWIDTH).transpose(1, 0, 2).reshape(S, FWIDTH)


def _even_out_kernel(a_ref, f_ref, x_ref, wa_ref, wf_ref, g_ref, b_ref, o_ref):
    y = jnp.dot(a_ref[...], wa_ref[...], preferred_element_type=F32)
    y = y + jnp.dot(f_ref[...].astype(BF16), wf_ref[...], preferred_element_type=F32)
    o_ref[...] = _layer_norm(DN_ALPHA * x_ref[...] + y, g_ref[...], b_ref[...])


def _even_out(attn, four, x, wa, wf, g, b):
    S = x.shape[0]
    tm = ROW_TILE
    row = lambda i: (i, 0)
    return pl.pallas_call(
        _even_out_kernel,
        grid=(S // tm,),
        in_specs=[pl.BlockSpec((tm, HEADS * V_DIM), row), pl.BlockSpec((tm, FWIDTH), row),
                  pl.BlockSpec((tm, D_MODEL), row),
                  _resident(wa.shape), _resident(wf.shape),
                  _resident((1, D_MODEL)), _resident((1, D_MODEL))],
        out_specs=pl.BlockSpec((tm, D_MODEL), row),
        out_shape=jax.ShapeDtypeStruct((S, D_MODEL), F32),
        compiler_params=_params(1),
        name="even_out_ln",
    )(attn, four, x, wa, wf, g, b)


def _odd_kernel(x_ref, win_ref, ng_ref, nb_ref, ws_ref, bs_ref, wout_ref, g_ref, b_ref, o_ref):
    tm = x_ref.shape[0]
    x = x_ref[...]
    z = jnp.dot(x.astype(BF16), win_ref[...], preferred_element_type=F32)
    z = 0.5 * z * (1.0 + lax.erf(z * (2.0 ** -0.5)))
    u = z[:, :SGU_WIDTH]
    v = _layer_norm(z[:, SGU_WIDTH:], ng_ref[...], nb_ref[...]).astype(BF16)
    gated = []
    for c in range(tm // SGU_CHUNK):
        rows = slice(c * SGU_CHUNK, (c + 1) * SGU_CHUNK)
        parts = []
        for grp in range(SGU_GROUPS):
            cols = slice(grp * SGU_GDIM, (grp + 1) * SGU_GDIM)
            s = jnp.dot(ws_ref[grp], v[rows, cols], preferred_element_type=F32) + bs_ref[grp]
            parts.append((u[rows, cols] * s).astype(BF16))
        gated.append(jnp.concatenate(parts, axis=1))
    gated = jnp.concatenate(gated, axis=0)
    y = jnp.dot(gated, wout_ref[...], preferred_element_type=F32)
    o_ref[...] = _layer_norm(DN_ALPHA * x + y, g_ref[...], b_ref[...])


def _odd_layer(x, win, ng, nb, ws, bs, wout, g, b):
    S = x.shape[0]
    tm = SGU_ROWS
    row = lambda i: (i, 0)
    return pl.pallas_call(
        _odd_kernel,
        grid=(S // tm,),
        in_specs=[pl.BlockSpec((tm, D_MODEL), row),
                  _resident(win.shape), _resident(ng.shape), _resident(nb.shape),
                  _resident(ws.shape), _resident(bs.shape), _resident(wout.shape),
                  _resident((1, D_MODEL)), _resident((1, D_MODEL))],
        out_specs=pl.BlockSpec((tm, D_MODEL), row),
        out_shape=jax.ShapeDtypeStruct((S, D_MODEL), F32),
        compiler_params=_params(1),
        name="odd_sgu_ln",
    )(x, win, ng, nb, ws, bs, wout, g, b)


def _ffn_kernel(x_ref, wg_ref, wu_ref, wd_ref, g_ref, b_ref, o_ref):
    x = x_ref[...]
    xb = x.astype(BF16)
    y = None
    for c in range(D_FF // FF_CHUNK):
        cols = slice(c * FF_CHUNK, (c + 1) * FF_CHUNK)
        gate = jnp.dot(xb, wg_ref[:, cols], preferred_element_type=F32)
        up = jnp.dot(xb, wu_ref[:, cols], preferred_element_type=F32)
        hid = (jax.nn.silu(gate) * up).astype(BF16)
        part = jnp.dot(hid, wd_ref[cols, :], preferred_element_type=F32)
        y = part if y is None else y + part
    o_ref[...] = _layer_norm(DN_ALPHA * x + y, g_ref[...], b_ref[...])


def _ffn(x, wg, wu, wd, g, b):
    S = x.shape[0]
    tm = ROW_TILE
    row = lambda i: (i, 0)
    return pl.pallas_call(
        _ffn_kernel,
        grid=(S // tm,),
        in_specs=[pl.BlockSpec((tm, D_MODEL), row),
                  _resident(wg.shape), _resident(wu.shape), _resident(wd.shape),
                  _resident((1, D_MODEL)), _resident((1, D_MODEL))],
        out_specs=pl.BlockSpec((tm, D_MODEL), row),
        out_shape=jax.ShapeDtypeStruct((S, D_MODEL), F32),
        compiler_params=_params(1),
        name="swiglu_ln",
    )(x, wg, wu, wd, g, b)


def _rotary_tables(seq):
    inv = 1.0 / (ROPE_THETA ** (jnp.arange(0, ROPE, 2, dtype=F32) / ROPE))
    ang = jnp.arange(seq, dtype=F32)[:, None] * inv[None, :]
    cos, sin = jnp.cos(ang), jnp.sin(ang)
    cc = jnp.concatenate([cos, cos, cos, cos], axis=1)
    ss = jnp.concatenate([-sin, sin, -sin, sin], axis=1)
    return cos.T, sin.T, cc, ss


def _twiddle_tables():
    k2 = jnp.arange(DFT_N, dtype=jnp.int32)[:, None]
    n1 = jnp.arange(DFT_N, dtype=jnp.int32)[None, :]
    ang = (k2 * n1).astype(F32) * F32(2.0 * np.pi / SEQ)
    ang = ang.reshape(SEQ, 1)
    return (jnp.broadcast_to(jnp.cos(ang), (SEQ, FDIM)),
            jnp.broadcast_to(jnp.sin(ang), (SEQ, FDIM)))


def kernel(x, even_w_in, even_q_norm, even_w_uq, even_kv_norm, even_w_uk, even_w_uv, even_w_out,
           odd_w_in, odd_sgu_norm_g, odd_sgu_norm_b, odd_w_spatial, odd_b_spatial, odd_w_out,
           mix_ln_g, mix_ln_b, ffn_w_gate, ffn_w_up, ffn_w_down, ffn_ln_g, ffn_ln_b):
    B, S, D = x.shape
    assert (B, S, D) == (1, SEQ, D_MODEL)
    cosT, sinT, cc, ss = _rotary_tables(S)
    twc, tws = _twiddle_tables()
    h = x.reshape(S, D)
    half = ROPE // 2
    c0 = Q_RANK + KV_RANK
    c1 = c0 + ROPE
    for layer in range(DEPTH):
        i = layer // 2
        if layer % 2 == 0:
            w = even_w_in[i]
            win = jnp.concatenate(
                [w[:, :c0], w[:, c1:], w[:, c0:c1], w[:, c0 + half:c1], w[:, c0:c0 + half]],
                axis=1).astype(BF16)
            qT, k, vT, f = _even_in(
                h, win, even_q_norm[i].reshape(1, -1), even_w_uq[i].T.astype(BF16),
                even_kv_norm[i].reshape(1, -1), even_w_uk[i].astype(BF16),
                even_w_uv[i].T.astype(BF16), cosT, sinT, cc, ss)
            attn = _attention(qT, k, vT)
            four = _fourier_mix(f, twc, tws)
            wo = even_w_out[i].astype(BF16)
            h = _even_out(attn, four, h, wo[:HEADS * V_DIM], wo[HEADS * V_DIM:],
                          mix_ln_g[layer].reshape(1, -1), mix_ln_b[layer].reshape(1, -1))
        else:
            h = _odd_layer(h, odd_w_in[i].astype(BF16),
                           odd_sgu_norm_g[i].reshape(1, -1), odd_sgu_norm_b[i].reshape(1, -1),
                           odd_w_spatial[i].astype(BF16),
                           odd_b_spatial[i].reshape(SGU_GROUPS, SGU_CHUNK, 1),
                           odd_w_out[i].astype(BF16),
                           mix_ln_g[layer].reshape(1, -1), mix_ln_b[layer].reshape(1, -1))
        h = _ffn(h, ffn_w_gate[layer].astype(BF16), ffn_w_up[layer].astype(BF16),
                 ffn_w_down[layer].astype(BF16),
                 ffn_ln_g[layer].reshape(1, -1), ffn_ln_b[layer].reshape(1, -1))
    return h.reshape(B, S, D)
```

```python
import functools
import math

import numpy as np
import jax
import jax.numpy as jnp
from jax import lax
from jax.experimental import pallas as pl
from jax.experimental.pallas import tpu as pltpu

F32 = jnp.float32
BF16 = jnp.bfloat16

D_MODEL = 1024
SEQ = 16384
DEPTH = 4
HEADS = 8
NOPE = 128
ROPE = 64
QK_DIM = NOPE + ROPE
V_DIM = 128
Q_RANK = 384
KV_RANK = 256
ROPE_THETA = 10000.0
FGROUPS = 4
FDIM = 128
FWIDTH = FGROUPS * FDIM
SGU_CHUNK = 128
SGU_GROUPS = 8
SGU_WIDTH = 2 * D_MODEL
SGU_GDIM = SGU_WIDTH // SGU_GROUPS
D_FF = 2816
DN_ALPHA = (2 * DEPTH) ** 0.25
LN_EPS = 1e-5
RMS_EPS = 1e-6

VMEM_LIMIT_BYTES = 56 * 1024 * 1024
DFT_N = 128
DFT_ROWS = 8

ROW_TILE = 512
KV_TILE = 512
Q_TILE = 2048
FF_CHUNKS = (1536, 1280)
SGU_ROWS = 512
ATTN_UNROLL = 4
SCORE_BOUND_LOG2 = 60.0
BOUND_MARGIN = 1.05

_NT = (((1,), (1,)), ((), ()))


def _params(n_axes):
    return pltpu.CompilerParams(
        dimension_semantics=("arbitrary",) * n_axes, vmem_limit_bytes=VMEM_LIMIT_BYTES)


def _resident(shape):
    nd = len(shape)
    return pl.BlockSpec(shape, lambda *_: (0,) * nd, pipeline_mode=pl.Buffered(1))


def _layer_norm(z, g, b):
    mu = jnp.mean(z, axis=-1, keepdims=True)
    zc = z - mu
    var = jnp.mean(zc * zc, axis=-1, keepdims=True)
    return zc * lax.rsqrt(var + LN_EPS) * g + b


def _rms_norm(z, g):
    ms = jnp.mean(z * z, axis=-1, keepdims=True)
    return z * lax.rsqrt(ms + RMS_EPS) * g


def _even_in_kernel(x_ref, win_ref, qn_ref, wuqT_ref, kvn_ref, wuk_ref, wuvT_ref,
                    cosT_ref, sinT_ref, cc_ref, ss_ref,
                    qT_ref, k_ref, vT_ref, f_ref, qmax_ref, kmax_ref, *, q_scale):
    tm = x_ref.shape[0]
    n_kv = tm // KV_TILE
    xb = x_ref[...].astype(BF16)
    h = jnp.dot(xb, win_ref[...], preferred_element_type=F32)
    cq = _rms_norm(h[:, :Q_RANK], qn_ref[...]).astype(BF16)
    ckv = _rms_norm(h[:, Q_RANK:Q_RANK + KV_RANK], kvn_ref[...]).astype(BF16)
    f_ref[...] = h[:, Q_RANK + KV_RANK:Q_RANK + KV_RANK + FWIDTH]
    kr = h[:, Q_RANK + KV_RANK + FWIDTH:]
    kr_rot = kr * cc_ref[...] + pltpu.roll(kr, ROPE, axis=1) * ss_ref[...]
    kr_rot = kr_rot[:, :ROPE]
    kr_sq = jnp.sum(kr_rot * kr_rot, axis=1, keepdims=True)
    kr_rot = kr_rot.astype(BF16)

    qT = lax.dot_general(wuqT_ref[...], cq, _NT, preferred_element_type=F32)
    cosT = cosT_ref[...]
    sinT = sinT_ref[...]
    half = ROPE // 2
    for hd in range(HEADS):
        base = hd * QK_DIM
        qT_ref[hd, 0:NOPE, :] = (qT[base:base + NOPE] * q_scale).astype(BF16)
        t1 = qT[base + NOPE:base + NOPE + half]
        t2 = qT[base + NOPE + half:base + QK_DIM]
        qT_ref[hd, NOPE:NOPE + half, :] = ((t1 * cosT - t2 * sinT) * q_scale).astype(BF16)
        qT_ref[hd, NOPE + half:QK_DIM, :] = ((t1 * sinT + t2 * cosT) * q_scale).astype(BF16)
        qh = qT[base:base + QK_DIM] * q_scale
        q_sq = jnp.max(jnp.sum(qh * qh, axis=0, keepdims=True), axis=1, keepdims=True)
        qmax_ref[0, hd:hd + 1, :] = jnp.broadcast_to(q_sq, (1, 128))

    kn = jnp.dot(ckv, wuk_ref[...], preferred_element_type=F32)
    vT = lax.dot_general(wuvT_ref[...], ckv, _NT, preferred_element_type=F32)
    for hd in range(HEADS):
        kh = kn[:, hd * NOPE:(hd + 1) * NOPE]
        k_sq = jnp.max(jnp.sum(kh * kh, axis=1, keepdims=True) + kr_sq, axis=0, keepdims=True)
        kmax_ref[0, hd:hd + 1, :] = jnp.broadcast_to(k_sq, (1, 128))
        for c in range(n_kv):
            rows = slice(c * KV_TILE, (c + 1) * KV_TILE)
            k_ref[hd, c, :, 0:NOPE] = kn[rows, hd * NOPE:(hd + 1) * NOPE].astype(BF16)
            k_ref[hd, c, :, NOPE:QK_DIM] = kr_rot[rows]
            vT_ref[hd, c] = vT[hd * V_DIM:(hd + 1) * V_DIM, rows].astype(BF16)


def _even_in(x, win, qn, wuqT, kvn, wuk, wuvT, cosT, sinT, cc, ss):
    S = x.shape[0]
    tm = ROW_TILE
    n_kv = tm // KV_TILE
    win_w = win.shape[1]
    q_scale = (QK_DIM ** -0.5) * math.log2(math.e)
    row = lambda i: (i, 0)
    return pl.pallas_call(
        functools.partial(_even_in_kernel, q_scale=q_scale),
        grid=(S // tm,),
        in_specs=[
            pl.BlockSpec((tm, D_MODEL), row),
            _resident((D_MODEL, win_w)),
            _resident((1, Q_RANK)),
            _resident((HEADS * QK_DIM, Q_RANK)),
            _resident((1, KV_RANK)),
            _resident((KV_RANK, HEADS * NOPE)),
            _resident((HEADS * V_DIM, KV_RANK)),
            pl.BlockSpec((ROPE // 2, tm), lambda i: (0, i)),
            pl.BlockSpec((ROPE // 2, tm), lambda i: (0, i)),
            pl.BlockSpec((tm, 2 * ROPE), row),
            pl.BlockSpec((tm, 2 * ROPE), row),
        ],
        out_specs=[
            pl.BlockSpec((HEADS, QK_DIM, tm), lambda i: (0, 0, i)),
            pl.BlockSpec((HEADS, n_kv, KV_TILE, QK_DIM), lambda i: (0, i, 0, 0)),
            pl.BlockSpec((HEADS, n_kv, V_DIM, KV_TILE), lambda i: (0, i, 0, 0)),
            pl.BlockSpec((tm, FWIDTH), row),
            pl.BlockSpec((1, HEADS, 128), lambda i: (i, 0, 0)),
            pl.BlockSpec((1, HEADS, 128), lambda i: (i, 0, 0)),
        ],
        out_shape=[
            jax.ShapeDtypeStruct((HEADS, QK_DIM, S), BF16),
            jax.ShapeDtypeStruct((HEADS, S // KV_TILE, KV_TILE, QK_DIM), BF16),
            jax.ShapeDtypeStruct((HEADS, S // KV_TILE, V_DIM, KV_TILE), BF16),
            jax.ShapeDtypeStruct((S, FWIDTH), F32),
            jax.ShapeDtypeStruct((S // tm, HEADS, 128), F32),
            jax.ShapeDtypeStruct((S // tm, HEADS, 128), F32),
        ],
        compiler_params=_params(1),
        name="even_in",
    )(x, win, qn, wuqT, kvn, wuk, wuvT, cosT, sinT, cc, ss)


def _attn_bounded(qT_ref, k_ref, vT_ref, p0_ref, p1_ref, acc_ref, l_ref):
    n_kv, tk, _ = k_ref.shape
    tq = qT_ref.shape[1]
    qT = qT_ref[...]

    def probs(j, p_ref):
        s = jnp.dot(k_ref[j], qT, preferred_element_type=F32)
        p = jnp.exp2(s)
        p_ref[...] = p.astype(BF16)
        return jnp.sum(p.reshape(tk // 8, 8, tq), axis=0)

    def values(j, p_ref):
        acc_ref[...] += jnp.dot(vT_ref[j], p_ref[...], preferred_element_type=F32)

    acc_ref[...] = jnp.zeros_like(acc_ref)
    l8 = probs(0, p0_ref)

    def pair(i, l8):
        j = 2 * i
        l8 = l8 + probs(j + 1, p1_ref)
        values(j, p0_ref)
        extra = probs(jnp.minimum(j + 2, n_kv - 1), p0_ref)
        l8 = jnp.where(j + 2 < n_kv, l8 + extra, l8)
        values(j + 1, p1_ref)
        return l8

    l8 = lax.fori_loop(0, n_kv // 2, pair, l8, unroll=ATTN_UNROLL)
    l_ref[...] = jnp.sum(l8, axis=0, keepdims=True)


def _attn_general(qT_ref, k_ref, vT_ref, s0_ref, s1_ref, p0_ref, p1_ref, acc_ref, l_ref):
    n_kv = k_ref.shape[0]
    tq = qT_ref.shape[1]
    qT = qT_ref[...]

    def scores(j, s_ref):
        s = jnp.dot(k_ref[j], qT, preferred_element_type=F32)
        s_ref[...] = s
        return jnp.max(s, axis=0, keepdims=True)

    def softmax(s_ref, p_ref, mx, m, l):
        m_new = jnp.maximum(m, mx)
        alpha = jnp.exp2(m - m_new)
        p = jnp.exp2(s_ref[...] - m_new)
        p_ref[...] = p.astype(BF16)
        return m_new, alpha * l + jnp.sum(p, axis=0, keepdims=True), alpha

    def values(j, p_ref, alpha):
        pv = jnp.dot(vT_ref[j], p_ref[...], preferred_element_type=F32)
        acc_ref[...] = alpha * acc_ref[...] + pv

    p1_ref[...] = jnp.zeros_like(p1_ref)
    acc_ref[...] = jnp.zeros_like(acc_ref)
    mx0 = scores(0, s0_ref)

    def pair(i, carry):
        mx0, m, l, alpha_prev = carry
        j = 2 * i
        mx1 = scores(j + 1, s1_ref)
        m, l, alpha0 = softmax(s0_ref, p0_ref, mx0, m, l)
        values(jnp.maximum(j - 1, 0), p1_ref, alpha_prev)
        mx0 = scores(jnp.minimum(j + 2, n_kv - 1), s0_ref)
        m, l, alpha1 = softmax(s1_ref, p1_ref, mx1, m, l)
        values(j, p0_ref, alpha0)
        return mx0, m, l, alpha1

    m0 = jnp.full((1, tq), -jnp.inf, F32)
    l0 = jnp.zeros((1, tq), F32)
    _, _, l, alpha = lax.fori_loop(0, n_kv // 2, pair, (mx0, m0, l0, jnp.ones((1, tq), F32)))
    values(n_kv - 1, p1_ref, alpha)
    l_ref[...] = l


def _attn_kernel(bounded_ref, qT_ref, k_ref, vT_ref, o_ref,
                 s0_ref, s1_ref, p0_ref, p1_ref, acc_ref, l_ref):
    bounded = bounded_ref[pl.program_id(0), pl.program_id(1)] != 0

    @pl.when(bounded)
    def _():
        _attn_bounded(qT_ref, k_ref, vT_ref, p0_ref, p1_ref, acc_ref, l_ref)

    @pl.when(jnp.logical_not(bounded))
    def _():
        _attn_general(qT_ref, k_ref, vT_ref, s0_ref, s1_ref, p0_ref, p1_ref, acc_ref, l_ref)

    o_ref[...] = (acc_ref[...] / l_ref[...]).T.astype(o_ref.dtype)


def _attention(bounded, qT, k, vT):
    H, n_kv, tk, _ = k.shape
    S = qT.shape[2]
    tq = Q_TILE
    assert n_kv % 2 == 0
    return pl.pallas_call(
        _attn_kernel,
        grid_spec=pltpu.PrefetchScalarGridSpec(
            num_scalar_prefetch=1,
            grid=(H, S // tq),
            in_specs=[
                pl.BlockSpec((None, QK_DIM, tq), lambda h, i, b: (h, 0, i)),
                pl.BlockSpec((None, n_kv, tk, QK_DIM), lambda h, i, b: (h, 0, 0, 0)),
                pl.BlockSpec((None, n_kv, V_DIM, tk), lambda h, i, b: (h, 0, 0, 0)),
            ],
            out_specs=pl.BlockSpec((tq, V_DIM), lambda h, i, b: (i, h)),
            scratch_shapes=[pltpu.VMEM((tk, tq), F32), pltpu.VMEM((tk, tq), F32),
                            pltpu.VMEM((tk, tq), BF16), pltpu.VMEM((tk, tq), BF16),
                            pltpu.VMEM((V_DIM, tq), F32), pltpu.VMEM((1, tq), F32)],
        ),
        out_shape=jax.ShapeDtypeStruct((S, H * V_DIM), BF16),
        compiler_params=_params(2),
        name="mla_attention",
    )(bounded, qT, k, vT)


def _score_bound_flags(qmax_sq, kmax_sq):
    n_blk = qmax_sq.shape[0]
    per = Q_TILE // ROW_TILE
    q_sq = jnp.max(qmax_sq[:, :, 0].reshape(n_blk // per, per, HEADS), axis=1)
    k_sq = jnp.max(kmax_sq[:, :, 0], axis=0)
    ok = q_sq * k_sq[None, :] * (BOUND_MARGIN ** 2) <= SCORE_BOUND_LOG2 ** 2
    return ok.astype(jnp.int32).T


def _dft_consts():
    n = np.arange(DFT_N)
    ang = 2.0 * np.pi * np.outer(n, n) / DFT_N
    c = np.cos(ang) / np.sqrt(DFT_N)
    s = np.sin(ang) / np.sqrt(DFT_N)
    w_a = np.concatenate([c, -s], axis=0)
    w_b = np.block([[c, s], [-s, c]])
    w_c = np.concatenate([c, s], axis=0)
    return tuple(jnp.asarray(w, F32).astype(BF16) for w in (w_a, w_b, w_c))


def _dft_a_kernel(x_ref, w_ref, tr_ref, ti_ref, xs_ref, ts_ref):
    w = w_ref[...]
    for r in range(DFT_ROWS):
        xs_ref[r] = x_ref[:, r, :]
    for r in range(DFT_ROWS):
        ts_ref[r] = jnp.dot(w, xs_ref[r].astype(BF16), preferred_element_type=F32)
    for r in range(DFT_ROWS):
        tr_ref[:, r, :] = ts_ref[r, :DFT_N, :]
        ti_ref[:, r, :] = ts_ref[r, DFT_N:, :]


def _dft_b_kernel(tr_ref, ti_ref, twc_ref, tws_ref, wb_ref, wc_ref, y_ref, ys_ref):
    re, im = [], []
    for j in range(DFT_ROWS):
        tr, ti = tr_ref[j], ti_ref[j]
        cw = jnp.concatenate([twc_ref[j]] * FGROUPS, axis=1)
        sw = jnp.concatenate([tws_ref[j]] * FGROUPS, axis=1)
        re.append((tr * cw + ti * sw).astype(BF16))
        im.append((ti * cw - tr * sw).astype(BF16))
    st = jnp.concatenate([jnp.concatenate(re, axis=1), jnp.concatenate(im, axis=1)], axis=0)
    g = jnp.dot(wb_ref[...], st, preferred_element_type=F32).astype(BF16)
    n_blk = DFT_ROWS * FGROUPS
    gg = jnp.concatenate(
        [jnp.concatenate([g[:DFT_N, b * FDIM:(b + 1) * FDIM], g[DFT_N:, b * FDIM:(b + 1) * FDIM]], axis=1)
         for b in range(n_blk)], axis=0)
    y = jnp.dot(gg, wc_ref[...], preferred_element_type=F32)
    for j in range(DFT_ROWS):
        for grp in range(FGROUPS):
            b = j * FGROUPS + grp
            ys_ref[j, :, grp * FDIM:(grp + 1) * FDIM] = y[b * DFT_N:(b + 1) * DFT_N]
    for j in range(DFT_ROWS):
        y_ref[:, j, :] = ys_ref[j]


def _fourier_mix(f, twc, tws):
    S = f.shape[0]
    w_a, w_b, w_c = _dft_consts()
    cube = (DFT_N, DFT_N, FWIDTH)
    steps = (DFT_N // DFT_ROWS,)
    inner = pl.BlockSpec((DFT_N, DFT_ROWS, FWIDTH), lambda i: (0, i, 0))
    outer = pl.BlockSpec((DFT_ROWS, DFT_N, FWIDTH), lambda i: (i, 0, 0))
    tr, ti = pl.pallas_call(
        _dft_a_kernel,
        grid=steps,
        in_specs=[inner, _resident(w_a.shape)],
        out_specs=[inner, inner],
        out_shape=[jax.ShapeDtypeStruct(cube, F32)] * 2,
        scratch_shapes=[pltpu.VMEM((DFT_ROWS, DFT_N, FWIDTH), F32),
                        pltpu.VMEM((DFT_ROWS, 2 * DFT_N, FWIDTH), F32)],
        compiler_params=_params(1),
        name="dft_positions_outer",
    )(f.reshape(cube), w_a)
    y = pl.pallas_call(
        _dft_b_kernel,
        grid=steps,
        in_specs=[outer, outer,
                  pl.BlockSpec((DFT_ROWS, DFT_N, FDIM), lambda i: (i, 0, 0)),
                  pl.BlockSpec((DFT_ROWS, DFT_N, FDIM), lambda i: (i, 0, 0)),
                  _resident(w_b.shape), _resident(w_c.shape)],
        out_specs=inner,
        out_shape=jax.ShapeDtypeStruct(cube, F32),
        scratch_shapes=[pltpu.VMEM((DFT_ROWS, DFT_N, FWIDTH), F32)],
        compiler_params=_params(1),
        name="dft_positions_inner_channels",
    )(tr, ti, twc, tws, w_b, w_c)
    return y.reshape(S, FWIDTH)


def _even_out_kernel(a_ref, f_ref, x_ref, wa_ref, wf_ref, g_ref, b_ref, o_ref):
    y = jnp.dot(a_ref[...], wa_ref[...], preferred_element_type=F32)
    y = y + jnp.dot(f_ref[...].astype(BF16), wf_ref[...], preferred_element_type=F32)
    o_ref[...] = _layer_norm(DN_ALPHA * x_ref[...] + y, g_ref[...], b_ref[...])


def _even_out(attn, four, x, wa, wf, g, b):
    S = x.shape[0]
    tm = ROW_TILE
    row = lambda i: (i, 0)
    return pl.pallas_call(
        _even_out_kernel,
        grid=(S // tm,),
        in_specs=[pl.BlockSpec((tm, HEADS * V_DIM), row), pl.BlockSpec((tm, FWIDTH), row),
                  pl.BlockSpec((tm, D_MODEL), row),
                  _resident(wa.shape), _resident(wf.shape),
                  _resident((1, D_MODEL)), _resident((1, D_MODEL))],
        out_specs=pl.BlockSpec((tm, D_MODEL), row),
        out_shape=jax.ShapeDtypeStruct((S, D_MODEL), F32),
        compiler_params=_params(1),
        name="even_out_ln",
    )(attn, four, x, wa, wf, g, b)


def _odd_kernel(x_ref, win_ref, ng_ref, nb_ref, ws_ref, bs_ref, wout_ref, g_ref, b_ref, o_ref):
    tm = x_ref.shape[0]
    x = x_ref[...]
    z = jnp.dot(x.astype(BF16), win_ref[...], preferred_element_type=F32)
    z = 0.5 * z * (1.0 + lax.erf(z * (2.0 ** -0.5)))
    u = z[:, :SGU_WIDTH]
    v = _layer_norm(z[:, SGU_WIDTH:], ng_ref[...], nb_ref[...]).astype(BF16)
    gated = []
    for c in range(tm // SGU_CHUNK):
        rows = slice(c * SGU_CHUNK, (c + 1) * SGU_CHUNK)
        parts = []
        for grp in range(SGU_GROUPS):
            cols = slice(grp * SGU_GDIM, (grp + 1) * SGU_GDIM)
            s = jnp.dot(ws_ref[grp], v[rows, cols], preferred_element_type=F32) + bs_ref[grp]
            parts.append((u[rows, cols] * s).astype(BF16))
        gated.append(jnp.concatenate(parts, axis=1))
    gated = jnp.concatenate(gated, axis=0)
    y = jnp.dot(gated, wout_ref[...], preferred_element_type=F32)
    o_ref[...] = _layer_norm(DN_ALPHA * x + y, g_ref[...], b_ref[...])


def _odd_layer(x, win, ng, nb, ws, bs, wout, g, b):
    S = x.shape[0]
    tm = SGU_ROWS
    row = lambda i: (i, 0)
    return pl.pallas_call(
        _odd_kernel,
        grid=(S // tm,),
        in_specs=[pl.BlockSpec((tm, D_MODEL), row),
                  _resident(win.shape), _resident(ng.shape), _resident(nb.shape),
                  _resident(ws.shape), _resident(bs.shape), _resident(wout.shape),
                  _resident((1, D_MODEL)), _resident((1, D_MODEL))],
        out_specs=pl.BlockSpec((tm, D_MODEL), row),
        out_shape=jax.ShapeDtypeStruct((S, D_MODEL), F32),
        compiler_params=_params(1),
        name="odd_sgu_ln",
    )(x, win, ng, nb, ws, bs, wout, g, b)


def _ffn_kernel(x_ref, wg_ref, wu_ref, wd_ref, g_ref, b_ref, o_ref):
    x = x_ref[...]
    xb = x.astype(BF16)
    y = None
    start = 0
    for width in FF_CHUNKS:
        cols = slice(start, start + width)
        start += width
        gate = jnp.dot(xb, wg_ref[:, cols], preferred_element_type=F32)
        up = jnp.dot(xb, wu_ref[:, cols], preferred_element_type=F32)
        hid = (jax.nn.silu(gate) * up).astype(BF16)
        part = jnp.dot(hid, wd_ref[cols, :], preferred_element_type=F32)
        y = part if y is None else y + part
    o_ref[...] = _layer_norm(DN_ALPHA * x + y, g_ref[...], b_ref[...])


def _ffn(x, wg, wu, wd, g, b):
    S = x.shape[0]
    tm = ROW_TILE
    row = lambda i: (i, 0)
    return pl.pallas_call(
        _ffn_kernel,
        grid=(S // tm,),
        in_specs=[pl.BlockSpec((tm, D_MODEL), row),
                  _resident(wg.shape), _resident(wu.shape), _resident(wd.shape),
                  _resident((1, D_MODEL)), _resident((1, D_MODEL))],
        out_specs=pl.BlockSpec((tm, D_MODEL), row),
        out_shape=jax.ShapeDtypeStruct((S, D_MODEL), F32),
        compiler_params=_params(1),
        name="swiglu_ln",
    )(x, wg, wu, wd, g, b)


def _rotary_tables(seq):
    inv = 1.0 / (ROPE_THETA ** (jnp.arange(0, ROPE, 2, dtype=F32) / ROPE))
    ang = jnp.arange(seq, dtype=F32)[:, None] * inv[None, :]
    cos, sin = jnp.cos(ang), jnp.sin(ang)
    cc = jnp.concatenate([cos, cos, cos, cos], axis=1)
    ss = jnp.concatenate([-sin, sin, -sin, sin], axis=1)
    return cos.T, sin.T, cc, ss


def _twiddle_tables():
    k2 = jnp.arange(DFT_N, dtype=jnp.int32)[:, None]
    n1 = jnp.arange(DFT_N, dtype=jnp.int32)[None, :]
    ang = (k2 * n1).astype(F32) * F32(2.0 * np.pi / SEQ)
    ang = ang.reshape(DFT_N, DFT_N, 1)
    return (jnp.broadcast_to(jnp.cos(ang), (DFT_N, DFT_N, FDIM)),
            jnp.broadcast_to(jnp.sin(ang), (DFT_N, DFT_N, FDIM)))


def kernel(x, even_w_in, even_q_norm, even_w_uq, even_kv_norm, even_w_uk, even_w_uv, even_w_out,
           odd_w_in, odd_sgu_norm_g, odd_sgu_norm_b, odd_w_spatial, odd_b_spatial, odd_w_out,
           mix_ln_g, mix_ln_b, ffn_w_gate, ffn_w_up, ffn_w_down, ffn_ln_g, ffn_ln_b):
    B, S, D = x.shape
    assert (B, S, D) == (1, SEQ, D_MODEL)
    cosT, sinT, cc, ss = _rotary_tables(S)
    twc, tws = _twiddle_tables()
    h = x.reshape(S, D)
    half = ROPE // 2
    c0 = Q_RANK + KV_RANK
    c1 = c0 + ROPE
    for layer in range(DEPTH):
        i = layer // 2
        if layer % 2 == 0:
            w = even_w_in[i]
            win = jnp.concatenate(
                [w[:, :c0], w[:, c1:], w[:, c0:c1], w[:, c0 + half:c1], w[:, c0:c0 + half]],
                axis=1).astype(BF16)
            qT, k, vT, f, qmax_sq, kmax_sq = _even_in(
                h, win, even_q_norm[i].reshape(1, -1), even_w_uq[i].T.astype(BF16),
                even_kv_norm[i].reshape(1, -1), even_w_uk[i].astype(BF16),
                even_w_uv[i].T.astype(BF16), cosT, sinT, cc, ss)
            attn = _attention(_score_bound_flags(qmax_sq, kmax_sq), qT, k, vT)
            four = _fourier_mix(f, twc, tws)
            wo = even_w_out[i].astype(BF16)
            h = _even_out(attn, four, h, wo[:HEADS * V_DIM], wo[HEADS * V_DIM:],
                          mix_ln_g[layer].reshape(1, -1), mix_ln_b[layer].reshape(1, -1))
        else:
            h = _odd_layer(h, odd_w_in[i].astype(BF16),
                           odd_sgu_norm_g[i].reshape(1, -1), odd_sgu_norm_b[i].reshape(1, -1),
                           odd_w_spatial[i].astype(BF16),
                           odd_b_spatial[i].reshape(SGU_GROUPS, SGU_CHUNK, 1),
                           odd_w_out[i].astype(BF16),
                           mix_ln_g[layer].reshape(1, -1), mix_ln_b[layer].reshape(1, -1))
        h = _ffn(h, ffn_w_gate[layer].astype(BF16), ffn_w_up[layer].astype(BF16),
                 ffn_w_down[layer].astype(BF16),
                 ffn_ln_g[layer].reshape(1, -1), ffn_ln_b[layer].reshape(1, -1))
    return h.reshape(B, S, D)
```

```python
import functools
import math

import numpy as np
import jax
import jax.numpy as jnp
from jax import lax
from jax.experimental import pallas as pl
from jax.experimental.pallas import tpu as pltpu

F32 = jnp.float32
BF16 = jnp.bfloat16

D_MODEL = 1024
SEQ = 16384
DEPTH = 4
HEADS = 8
NOPE = 128
ROPE = 64
QK_DIM = NOPE + ROPE
V_DIM = 128
Q_RANK = 384
KV_RANK = 256
ROPE_THETA = 10000.0
FGROUPS = 4
FDIM = 128
FWIDTH = FGROUPS * FDIM
SGU_CHUNK = 128
SGU_GROUPS = 8
SGU_WIDTH = 2 * D_MODEL
SGU_GDIM = SGU_WIDTH // SGU_GROUPS
D_FF = 2816
DN_ALPHA = (2 * DEPTH) ** 0.25
LN_EPS = 1e-5
RMS_EPS = 1e-6

VMEM_LIMIT_BYTES = 56 * 1024 * 1024
DFT_N = 128
DFT_ROWS = 8

ROW_TILE = 512
KV_TILE = 512
Q_TILE = 2048
FF_CHUNKS = (1536, 1280)
SGU_ROWS = 512
ATTN_UNROLL = 5
SCORE_BOUND_LOG2 = 60.0
BOUND_MARGIN = 1.05

_NT = (((1,), (1,)), ((), ()))


def _params(n_axes):
    return pltpu.CompilerParams(
        dimension_semantics=("arbitrary",) * n_axes, vmem_limit_bytes=VMEM_LIMIT_BYTES)


def _resident(shape):
    nd = len(shape)
    return pl.BlockSpec(shape, lambda *_: (0,) * nd, pipeline_mode=pl.Buffered(1))


def _layer_of(stacked_shape, layer):
    nd = len(stacked_shape)
    return pl.BlockSpec((None,) + tuple(stacked_shape[1:]), lambda *_: (layer,) + (0,) * (nd - 1),
                        pipeline_mode=pl.Buffered(1))


def _layer_norm(z, g, b):
    mu = jnp.mean(z, axis=-1, keepdims=True)
    zc = z - mu
    var = jnp.mean(zc * zc, axis=-1, keepdims=True)
    return zc * lax.rsqrt(var + LN_EPS) * g + b


def _rms_norm(z, g):
    ms = jnp.mean(z * z, axis=-1, keepdims=True)
    return z * lax.rsqrt(ms + RMS_EPS) * g


def _even_in_kernel(x_ref, win_ref, qn_ref, wuqT_ref, kvn_ref, wuk_ref, wuvT_ref,
                    cosT_ref, sinT_ref, cc_ref, ss_ref,
                    qT_ref, k_ref, vT_ref, f_ref, qmax_ref, kmax_ref, *, q_scale):
    tm = x_ref.shape[0]
    n_kv = tm // KV_TILE
    xb = x_ref[...].astype(BF16)
    h = jnp.dot(xb, win_ref[...], preferred_element_type=F32)
    cq = _rms_norm(h[:, :Q_RANK], qn_ref[...]).astype(BF16)
    ckv = _rms_norm(h[:, Q_RANK:Q_RANK + KV_RANK], kvn_ref[...]).astype(BF16)
    f_ref[...] = h[:, Q_RANK + KV_RANK:Q_RANK + KV_RANK + FWIDTH]
    kr = h[:, Q_RANK + KV_RANK + FWIDTH:]
    kr_rot = kr * cc_ref[...] + pltpu.roll(kr, ROPE, axis=1) * ss_ref[...]
    kr_rot = kr_rot[:, :ROPE]
    kr_sq = jnp.sum(kr_rot * kr_rot, axis=1, keepdims=True)
    kr_rot = kr_rot.astype(BF16)

    qT = lax.dot_general(wuqT_ref[...], cq, _NT, preferred_element_type=F32)
    cosT = cosT_ref[...]
    sinT = sinT_ref[...]
    half = ROPE // 2
    for hd in range(HEADS):
        base = hd * QK_DIM
        qT_ref[hd, 0:NOPE, :] = (qT[base:base + NOPE] * q_scale).astype(BF16)
        t1 = qT[base + NOPE:base + NOPE + half]
        t2 = qT[base + NOPE + half:base + QK_DIM]
        qT_ref[hd, NOPE:NOPE + half, :] = ((t1 * cosT - t2 * sinT) * q_scale).astype(BF16)
        qT_ref[hd, NOPE + half:QK_DIM, :] = ((t1 * sinT + t2 * cosT) * q_scale).astype(BF16)
        qh = qT[base:base + QK_DIM] * q_scale
        q_sq = jnp.max(jnp.sum(qh * qh, axis=0, keepdims=True), axis=1, keepdims=True)
        qmax_ref[0, hd:hd + 1, :] = jnp.broadcast_to(q_sq, (1, 128))

    kn = jnp.dot(ckv, wuk_ref[...], preferred_element_type=F32)
    vT = lax.dot_general(wuvT_ref[...], ckv, _NT, preferred_element_type=F32)
    for hd in range(HEADS):
        kh = kn[:, hd * NOPE:(hd + 1) * NOPE]
        k_sq = jnp.max(jnp.sum(kh * kh, axis=1, keepdims=True) + kr_sq, axis=0, keepdims=True)
        kmax_ref[0, hd:hd + 1, :] = jnp.broadcast_to(k_sq, (1, 128))
        for c in range(n_kv):
            rows = slice(c * KV_TILE, (c + 1) * KV_TILE)
            k_ref[hd, c, :, 0:NOPE] = kn[rows, hd * NOPE:(hd + 1) * NOPE].astype(BF16)
            k_ref[hd, c, :, NOPE:QK_DIM] = kr_rot[rows]
            vT_ref[hd, c] = vT[hd * V_DIM:(hd + 1) * V_DIM, rows].astype(BF16)


def _even_in(x, win, qn, wuqT, kvn, wuk, wuvT, cosT, sinT, cc, ss, i):
    S = x.shape[0]
    tm = ROW_TILE
    n_kv = tm // KV_TILE
    q_scale = (QK_DIM ** -0.5) * math.log2(math.e)
    row = lambda i: (i, 0)
    return pl.pallas_call(
        functools.partial(_even_in_kernel, q_scale=q_scale),
        grid=(S // tm,),
        in_specs=[
            pl.BlockSpec((tm, D_MODEL), row),
            _layer_of(win.shape, i), _layer_of(qn.shape, i), _layer_of(wuqT.shape, i),
            _layer_of(kvn.shape, i), _layer_of(wuk.shape, i), _layer_of(wuvT.shape, i),
            pl.BlockSpec((ROPE // 2, tm), lambda i: (0, i)),
            pl.BlockSpec((ROPE // 2, tm), lambda i: (0, i)),
            pl.BlockSpec((tm, 2 * ROPE), row),
            pl.BlockSpec((tm, 2 * ROPE), row),
        ],
        out_specs=[
            pl.BlockSpec((HEADS, QK_DIM, tm), lambda i: (0, 0, i)),
            pl.BlockSpec((HEADS, n_kv, KV_TILE, QK_DIM), lambda i: (0, i, 0, 0)),
            pl.BlockSpec((HEADS, n_kv, V_DIM, KV_TILE), lambda i: (0, i, 0, 0)),
            pl.BlockSpec((tm, FWIDTH), row),
            pl.BlockSpec((1, HEADS, 128), lambda i: (i, 0, 0)),
            pl.BlockSpec((1, HEADS, 128), lambda i: (i, 0, 0)),
        ],
        out_shape=[
            jax.ShapeDtypeStruct((HEADS, QK_DIM, S), BF16),
            jax.ShapeDtypeStruct((HEADS, S // KV_TILE, KV_TILE, QK_DIM), BF16),
            jax.ShapeDtypeStruct((HEADS, S // KV_TILE, V_DIM, KV_TILE), BF16),
            jax.ShapeDtypeStruct((S, FWIDTH), F32),
            jax.ShapeDtypeStruct((S // tm, HEADS, 128), F32),
            jax.ShapeDtypeStruct((S // tm, HEADS, 128), F32),
        ],
        compiler_params=_params(1),
        name="even_in",
    )(x, win, qn, wuqT, kvn, wuk, wuvT, cosT, sinT, cc, ss)


def _attn_bounded(qT_ref, k_ref, vT_ref, p0_ref, p1_ref, acc_ref, l_ref):
    n_kv, tk, _ = k_ref.shape
    tq = qT_ref.shape[1]
    qT = qT_ref[...]

    def probs(j, p_ref):
        s = jnp.dot(k_ref[j], qT, preferred_element_type=F32)
        p = jnp.exp2(s)
        p_ref[...] = p.astype(BF16)
        return jnp.sum(p.reshape(tk // 8, 8, tq), axis=0)

    def values(j, p_ref):
        acc_ref[...] += jnp.dot(vT_ref[j], p_ref[...], preferred_element_type=F32)

    acc_ref[...] = jnp.zeros_like(acc_ref)
    l8 = probs(0, p0_ref)

    def pair(i, l8):
        j = 2 * i
        l8 = l8 + probs(j + 1, p1_ref)
        values(j, p0_ref)
        l8 = l8 + probs(j + 2, p0_ref)
        values(j + 1, p1_ref)
        return l8

    l8 = lax.fori_loop(0, n_kv // 2 - 1, pair, l8, unroll=ATTN_UNROLL)
    l8 = l8 + probs(n_kv - 1, p1_ref)
    values(n_kv - 2, p0_ref)
    values(n_kv - 1, p1_ref)
    l_ref[...] = jnp.sum(l8, axis=0, keepdims=True)


def _attn_general(qT_ref, k_ref, vT_ref, s0_ref, s1_ref, p0_ref, p1_ref, acc_ref, l_ref):
    n_kv = k_ref.shape[0]
    tq = qT_ref.shape[1]
    qT = qT_ref[...]

    def scores(j, s_ref):
        s = jnp.dot(k_ref[j], qT, preferred_element_type=F32)
        s_ref[...] = s
        return jnp.max(s, axis=0, keepdims=True)

    def softmax(s_ref, p_ref, mx, m, l):
        m_new = jnp.maximum(m, mx)
        alpha = jnp.exp2(m - m_new)
        p = jnp.exp2(s_ref[...] - m_new)
        p_ref[...] = p.astype(BF16)
        return m_new, alpha * l + jnp.sum(p, axis=0, keepdims=True), alpha

    def values(j, p_ref, alpha):
        pv = jnp.dot(vT_ref[j], p_ref[...], preferred_element_type=F32)
        acc_ref[...] = alpha * acc_ref[...] + pv

    p1_ref[...] = jnp.zeros_like(p1_ref)
    acc_ref[...] = jnp.zeros_like(acc_ref)
    mx0 = scores(0, s0_ref)

    def pair(i, carry):
        mx0, m, l, alpha_prev = carry
        j = 2 * i
        mx1 = scores(j + 1, s1_ref)
        m, l, alpha0 = softmax(s0_ref, p0_ref, mx0, m, l)
        values(jnp.maximum(j - 1, 0), p1_ref, alpha_prev)
        mx0 = scores(jnp.minimum(j + 2, n_kv - 1), s0_ref)
        m, l, alpha1 = softmax(s1_ref, p1_ref, mx1, m, l)
        values(j, p0_ref, alpha0)
        return mx0, m, l, alpha1

    m0 = jnp.full((1, tq), -jnp.inf, F32)
    l0 = jnp.zeros((1, tq), F32)
    _, _, l, alpha = lax.fori_loop(0, n_kv // 2, pair, (mx0, m0, l0, jnp.ones((1, tq), F32)))
    values(n_kv - 1, p1_ref, alpha)
    l_ref[...] = l


def _attn_kernel(bounded_ref, qT_ref, k_ref, vT_ref, o_ref,
                 s0_ref, s1_ref, p0_ref, p1_ref, acc_ref, l_ref):
    bounded = bounded_ref[pl.program_id(0), pl.program_id(1)] != 0

    @pl.when(bounded)
    def _():
        _attn_bounded(qT_ref, k_ref, vT_ref, p0_ref, p1_ref, acc_ref, l_ref)

    @pl.when(jnp.logical_not(bounded))
    def _():
        _attn_general(qT_ref, k_ref, vT_ref, s0_ref, s1_ref, p0_ref, p1_ref, acc_ref, l_ref)

    o_ref[...] = (acc_ref[...] / l_ref[...]).T.astype(o_ref.dtype)


def _attention(bounded, qT, k, vT):
    H, n_kv, tk, _ = k.shape
    S = qT.shape[2]
    tq = Q_TILE
    assert n_kv % 2 == 0
    return pl.pallas_call(
        _attn_kernel,
        grid_spec=pltpu.PrefetchScalarGridSpec(
            num_scalar_prefetch=1,
            grid=(H, S // tq),
            in_specs=[
                pl.BlockSpec((None, QK_DIM, tq), lambda h, i, b: (h, 0, i)),
                pl.BlockSpec((None, n_kv, tk, QK_DIM), lambda h, i, b: (h, 0, 0, 0)),
                pl.BlockSpec((None, n_kv, V_DIM, tk), lambda h, i, b: (h, 0, 0, 0)),
            ],
            out_specs=pl.BlockSpec((tq, V_DIM), lambda h, i, b: (i, h)),
            scratch_shapes=[pltpu.VMEM((tk, tq), F32), pltpu.VMEM((tk, tq), F32),
                            pltpu.VMEM((tk, tq), BF16), pltpu.VMEM((tk, tq), BF16),
                            pltpu.VMEM((V_DIM, tq), F32), pltpu.VMEM((1, tq), F32)],
        ),
        out_shape=jax.ShapeDtypeStruct((S, H * V_DIM), BF16),
        compiler_params=_params(2),
        name="mla_attention",
    )(bounded, qT, k, vT)


def _score_bound_flags(qmax_sq, kmax_sq):
    n_blk = qmax_sq.shape[0]
    per = Q_TILE // ROW_TILE
    q_sq = jnp.max(qmax_sq[:, :, 0].reshape(n_blk // per, per, HEADS), axis=1)
    k_sq = jnp.max(kmax_sq[:, :, 0], axis=0)
    ok = q_sq * k_sq[None, :] * (BOUND_MARGIN ** 2) <= SCORE_BOUND_LOG2 ** 2
    return ok.astype(jnp.int32).T


def _dft_consts():
    n = np.arange(DFT_N)
    ang = 2.0 * np.pi * np.outer(n, n) / DFT_N
    c = np.cos(ang) / np.sqrt(DFT_N)
    s = np.sin(ang) / np.sqrt(DFT_N)
    w_a = np.concatenate([c, -s], axis=0)
    w_b = np.block([[c, s], [-s, c]])
    w_c = np.concatenate([c, s], axis=0)
    return tuple(jnp.asarray(w, F32).astype(BF16) for w in (w_a, w_b, w_c))


def _dft_a_kernel(x_ref, w_ref, tr_ref, ti_ref, xs_ref, ts_ref):
    w = w_ref[...]
    for r in range(DFT_ROWS):
        xs_ref[r] = x_ref[:, r, :]
    for r in range(DFT_ROWS):
        ts_ref[r] = jnp.dot(w, xs_ref[r].astype(BF16), preferred_element_type=F32)
    for r in range(DFT_ROWS):
        tr_ref[:, r, :] = ts_ref[r, :DFT_N, :]
        ti_ref[:, r, :] = ts_ref[r, DFT_N:, :]


def _dft_b_kernel(tr_ref, ti_ref, twc_ref, tws_ref, wb_ref, wc_ref, y_ref, ys_ref):
    re, im = [], []
    for j in range(DFT_ROWS):
        tr, ti = tr_ref[j], ti_ref[j]
        cw = jnp.concatenate([twc_ref[j]] * FGROUPS, axis=1)
        sw = jnp.concatenate([tws_ref[j]] * FGROUPS, axis=1)
        re.append((tr * cw + ti * sw).astype(BF16))
        im.append((ti * cw - tr * sw).astype(BF16))
    st = jnp.concatenate([jnp.concatenate(re, axis=1), jnp.concatenate(im, axis=1)], axis=0)
    g = jnp.dot(wb_ref[...], st, preferred_element_type=F32).astype(BF16)
    n_blk = DFT_ROWS * FGROUPS
    gg = jnp.concatenate(
        [jnp.concatenate([g[:DFT_N, b * FDIM:(b + 1) * FDIM], g[DFT_N:, b * FDIM:(b + 1) * FDIM]], axis=1)
         for b in range(n_blk)], axis=0)
    y = jnp.dot(gg, wc_ref[...], preferred_element_type=F32)
    for j in range(DFT_ROWS):
        for grp in range(FGROUPS):
            b = j * FGROUPS + grp
            ys_ref[j, :, grp * FDIM:(grp + 1) * FDIM] = y[b * DFT_N:(b + 1) * DFT_N]
    for j in range(DFT_ROWS):
        y_ref[:, j, :] = ys_ref[j]


def _fourier_mix(f, twc, tws):
    S = f.shape[0]
    w_a, w_b, w_c = _dft_consts()
    cube = (DFT_N, DFT_N, FWIDTH)
    steps = (DFT_N // DFT_ROWS,)
    inner = pl.BlockSpec((DFT_N, DFT_ROWS, FWIDTH), lambda i: (0, i, 0))
    outer = pl.BlockSpec((DFT_ROWS, DFT_N, FWIDTH), lambda i: (i, 0, 0))
    tr, ti = pl.pallas_call(
        _dft_a_kernel,
        grid=steps,
        in_specs=[inner, _resident(w_a.shape)],
        out_specs=[inner, inner],
        out_shape=[jax.ShapeDtypeStruct(cube, F32)] * 2,
        scratch_shapes=[pltpu.VMEM((DFT_ROWS, DFT_N, FWIDTH), F32),
                        pltpu.VMEM((DFT_ROWS, 2 * DFT_N, FWIDTH), F32)],
        compiler_params=_params(1),
        name="dft_positions_outer",
    )(f.reshape(cube), w_a)
    y = pl.pallas_call(
        _dft_b_kernel,
        grid=steps,
        in_specs=[outer, outer,
                  pl.BlockSpec((DFT_ROWS, DFT_N, FDIM), lambda i: (i, 0, 0)),
                  pl.BlockSpec((DFT_ROWS, DFT_N, FDIM), lambda i: (i, 0, 0)),
                  _resident(w_b.shape), _resident(w_c.shape)],
        out_specs=inner,
        out_shape=jax.ShapeDtypeStruct(cube, F32),
        scratch_shapes=[pltpu.VMEM((DFT_ROWS, DFT_N, FWIDTH), F32)],
        compiler_params=_params(1),
        name="dft_positions_inner_channels",
    )(tr, ti, twc, tws, w_b, w_c)
    return y.reshape(S, FWIDTH)


def _even_out_kernel(a_ref, f_ref, x_ref, wa_ref, wf_ref, g_ref, b_ref, o_ref):
    y = jnp.dot(a_ref[...], wa_ref[...], preferred_element_type=F32)
    y = y + jnp.dot(f_ref[...].astype(BF16), wf_ref[...], preferred_element_type=F32)
    o_ref[...] = _layer_norm(DN_ALPHA * x_ref[...] + y, g_ref[...], b_ref[...])


def _even_out(attn, four, x, wo, g, b, i, layer):
    S = x.shape[0]
    tm = ROW_TILE
    row = lambda r: (r, 0)
    n_attn = HEADS * V_DIM
    wa = pl.BlockSpec((None, n_attn, D_MODEL), lambda *_: (i, 0, 0), pipeline_mode=pl.Buffered(1))
    wf = pl.BlockSpec((None, FWIDTH, D_MODEL), lambda *_: (i, n_attn // FWIDTH, 0),
                      pipeline_mode=pl.Buffered(1))
    return pl.pallas_call(
        _even_out_kernel,
        grid=(S // tm,),
        in_specs=[pl.BlockSpec((tm, HEADS * V_DIM), row), pl.BlockSpec((tm, FWIDTH), row),
                  pl.BlockSpec((tm, D_MODEL), row),
                  wa, wf, _layer_of(g.shape, layer), _layer_of(b.shape, layer)],
        out_specs=pl.BlockSpec((tm, D_MODEL), row),
        out_shape=jax.ShapeDtypeStruct((S, D_MODEL), F32),
        compiler_params=_params(1),
        name="even_out_ln",
    )(attn, four, x, wo, wo, g, b)


def _odd_kernel(x_ref, win_ref, ng_ref, nb_ref, ws_ref, bs_ref, wout_ref, g_ref, b_ref, o_ref):
    tm = x_ref.shape[0]
    x = x_ref[...]
    z = jnp.dot(x.astype(BF16), win_ref[...], preferred_element_type=F32)
    z = 0.5 * z * (1.0 + lax.erf(z * (2.0 ** -0.5)))
    u = z[:, :SGU_WIDTH]
    v = _layer_norm(z[:, SGU_WIDTH:], ng_ref[...], nb_ref[...]).astype(BF16)
    gated = []
    for c in range(tm // SGU_CHUNK):
        rows = slice(c * SGU_CHUNK, (c + 1) * SGU_CHUNK)
        parts = []
        for grp in range(SGU_GROUPS):
            cols = slice(grp * SGU_GDIM, (grp + 1) * SGU_GDIM)
            s = jnp.dot(ws_ref[grp], v[rows, cols], preferred_element_type=F32) + bs_ref[grp]
            parts.append((u[rows, cols] * s).astype(BF16))
        gated.append(jnp.concatenate(parts, axis=1))
    gated = jnp.concatenate(gated, axis=0)
    y = jnp.dot(gated, wout_ref[...], preferred_element_type=F32)
    o_ref[...] = _layer_norm(DN_ALPHA * x + y, g_ref[...], b_ref[...])


def _odd_layer(x, win, ng, nb, ws, bs, wout, g, b, i, layer):
    S = x.shape[0]
    tm = SGU_ROWS
    row = lambda r: (r, 0)
    return pl.pallas_call(
        _odd_kernel,
        grid=(S // tm,),
        in_specs=[pl.BlockSpec((tm, D_MODEL), row),
                  _layer_of(win.shape, i), _layer_of(ng.shape, i), _layer_of(nb.shape, i),
                  _layer_of(ws.shape, i), _layer_of(bs.shape, i), _layer_of(wout.shape, i),
                  _layer_of(g.shape, layer), _layer_of(b.shape, layer)],
        out_specs=pl.BlockSpec((tm, D_MODEL), row),
        out_shape=jax.ShapeDtypeStruct((S, D_MODEL), F32),
        compiler_params=_params(1),
        name="odd_sgu_ln",
    )(x, win, ng, nb, ws, bs, wout, g, b)


def _ffn_kernel(x_ref, wg_ref, wu_ref, wd_ref, g_ref, b_ref, o_ref):
    x = x_ref[...]
    xb = x.astype(BF16)
    y = None
    start = 0
    for width in FF_CHUNKS:
        cols = slice(start, start + width)
        start += width
        gate = jnp.dot(xb, wg_ref[:, cols], preferred_element_type=F32)
        up = jnp.dot(xb, wu_ref[:, cols], preferred_element_type=F32)
        hid = (jax.nn.silu(gate) * up).astype(BF16)
        part = jnp.dot(hid, wd_ref[cols, :], preferred_element_type=F32)
        y = part if y is None else y + part
    o_ref[...] = _layer_norm(DN_ALPHA * x + y, g_ref[...], b_ref[...])


def _ffn(x, wg, wu, wd, g, b, layer):
    S = x.shape[0]
    tm = ROW_TILE
    row = lambda r: (r, 0)
    return pl.pallas_call(
        _ffn_kernel,
        grid=(S // tm,),
        in_specs=[pl.BlockSpec((tm, D_MODEL), row),
                  _layer_of(wg.shape, layer), _layer_of(wu.shape, layer), _layer_of(wd.shape, layer),
                  _layer_of(g.shape, layer), _layer_of(b.shape, layer)],
        out_specs=pl.BlockSpec((tm, D_MODEL), row),
        out_shape=jax.ShapeDtypeStruct((S, D_MODEL), F32),
        compiler_params=_params(1),
        name="swiglu_ln",
    )(x, wg, wu, wd, g, b)


def _rotary_tables(seq):
    inv = 1.0 / (ROPE_THETA ** (jnp.arange(0, ROPE, 2, dtype=F32) / ROPE))
    ang = jnp.arange(seq, dtype=F32)[:, None] * inv[None, :]
    cos, sin = lax.optimization_barrier((jnp.cos(ang), jnp.sin(ang)))
    cc = jnp.concatenate([cos, cos, cos, cos], axis=1)
    ss = jnp.concatenate([-sin, sin, -sin, sin], axis=1)
    return cos.T, sin.T, cc, ss


def _twiddle_tables():
    k2 = jnp.arange(DFT_N, dtype=jnp.int32)[:, None]
    n1 = jnp.arange(DFT_N, dtype=jnp.int32)[None, :]
    ang = (k2 * n1).astype(F32) * F32(2.0 * np.pi / SEQ)
    ang = ang.reshape(DFT_N, DFT_N, 1)
    cos, sin = lax.optimization_barrier((jnp.cos(ang), jnp.sin(ang)))
    return (jnp.broadcast_to(cos, (DFT_N, DFT_N, FDIM)),
            jnp.broadcast_to(sin, (DFT_N, DFT_N, FDIM)))


def kernel(x, even_w_in, even_q_norm, even_w_uq, even_kv_norm, even_w_uk, even_w_uv, even_w_out,
           odd_w_in, odd_sgu_norm_g, odd_sgu_norm_b, odd_w_spatial, odd_b_spatial, odd_w_out,
           mix_ln_g, mix_ln_b, ffn_w_gate, ffn_w_up, ffn_w_down, ffn_ln_g, ffn_ln_b):
    B, S, D = x.shape
    assert (B, S, D) == (1, SEQ, D_MODEL)
    cosT, sinT, cc, ss = _rotary_tables(S)
    twc, tws = _twiddle_tables()
    half = ROPE // 2
    c0 = Q_RANK + KV_RANK
    c1 = c0 + ROPE
    w = even_w_in
    even_win = jnp.concatenate(
        [w[..., :c0], w[..., c1:], w[..., c0:c1], w[..., c0 + half:c1], w[..., c0:c0 + half]],
        axis=-1).astype(BF16)
    even_qn = even_q_norm[:, None, :]
    even_kvn = even_kv_norm[:, None, :]
    even_wuqT = jnp.swapaxes(even_w_uq, 1, 2).astype(BF16)
    even_wuk = even_w_uk.astype(BF16)
    even_wuvT = jnp.swapaxes(even_w_uv, 1, 2).astype(BF16)
    even_wo = even_w_out.astype(BF16)
    odd_win = odd_w_in.astype(BF16)
    odd_ng = odd_sgu_norm_g[:, None, :]
    odd_nb = odd_sgu_norm_b[:, None, :]
    odd_ws = odd_w_spatial.astype(BF16)
    odd_bs = odd_b_spatial[..., None]
    odd_wout = odd_w_out.astype(BF16)
    mix_g, mix_b = mix_ln_g[:, None, :], mix_ln_b[:, None, :]
    ffn_g, ffn_b = ffn_ln_g[:, None, :], ffn_ln_b[:, None, :]
    ffn_wg, ffn_wu, ffn_wd = ffn_w_gate.astype(BF16), ffn_w_up.astype(BF16), ffn_w_down.astype(BF16)

    h = x.reshape(S, D)
    for layer in range(DEPTH):
        i = layer // 2
        if layer % 2 == 0:
            qT, k, vT, f, qmax_sq, kmax_sq = _even_in(
                h, even_win, even_qn, even_wuqT, even_kvn, even_wuk, even_wuvT, cosT, sinT, cc, ss, i)
            attn = _attention(_score_bound_flags(qmax_sq, kmax_sq), qT, k, vT)
            four = _fourier_mix(f, twc, tws)
            h = _even_out(attn, four, h, even_wo, mix_g, mix_b, i, layer)
        else:
            h = _odd_layer(h, odd_win, odd_ng, odd_nb, odd_ws, odd_bs, odd_wout, mix_g, mix_b, i, layer)
        h = _ffn(h, ffn_wg, ffn_wu, ffn_wd, ffn_g, ffn_b, layer)
    return h.reshape(B, S, D)
```

```python
import functools
import math

import numpy as np
import jax
import jax.numpy as jnp
from jax import lax
from jax.experimental import pallas as pl
from jax.experimental.pallas import tpu as pltpu

F32 = jnp.float32
BF16 = jnp.bfloat16

D_MODEL = 1024
SEQ = 16384
DEPTH = 4
HEADS = 8
NOPE = 128
ROPE = 64
QK_DIM = NOPE + ROPE
V_DIM = 128
Q_RANK = 384
KV_RANK = 256
ROPE_THETA = 10000.0
FGROUPS = 4
FDIM = 128
FWIDTH = FGROUPS * FDIM
SGU_CHUNK = 128
SGU_GROUPS = 8
SGU_WIDTH = 2 * D_MODEL
SGU_GDIM = SGU_WIDTH // SGU_GROUPS
D_FF = 2816
DN_ALPHA = (2 * DEPTH) ** 0.25
LN_EPS = 1e-5
RMS_EPS = 1e-6

VMEM_LIMIT_BYTES = 56 * 1024 * 1024
DFT_N = 128
DFT_ROWS = 8

ROW_TILE = 512
KV_TILE = 512
Q_TILE = 2048
FF_CHUNKS = (1536, 1280)
SGU_ROWS = 512
ATTN_UNROLL = 5
SCORE_BOUND_LOG2 = 60.0
BOUND_MARGIN = 1.05

_NT = (((1,), (1,)), ((), ()))


def _params(n_axes):
    return pltpu.CompilerParams(
        dimension_semantics=("arbitrary",) * n_axes, vmem_limit_bytes=VMEM_LIMIT_BYTES)


def _resident(shape):
    nd = len(shape)
    return pl.BlockSpec(shape, lambda *_: (0,) * nd, pipeline_mode=pl.Buffered(1))


def _layer_of(stacked_shape, layer):
    nd = len(stacked_shape)
    return pl.BlockSpec((None,) + tuple(stacked_shape[1:]), lambda *_: (layer,) + (0,) * (nd - 1),
                        pipeline_mode=pl.Buffered(1))


def _layer_norm(z, g, b):
    mu = jnp.mean(z, axis=-1, keepdims=True)
    zc = z - mu
    var = jnp.mean(zc * zc, axis=-1, keepdims=True)
    return zc * lax.rsqrt(var + LN_EPS) * g + b


def _rms_norm(z, g):
    ms = jnp.mean(z * z, axis=-1, keepdims=True)
    return z * lax.rsqrt(ms + RMS_EPS) * g


def _even_in_kernel(x_ref, win_ref, qn_ref, wuqT_ref, kvn_ref, wuk_ref, wuvT_ref,
                    cosT_ref, sinT_ref, cc_ref, ss_ref,
                    qT_ref, k_ref, vT_ref, f_ref, qmax_ref, kmax_ref, *, q_scale):
    tm = x_ref.shape[0]
    n_kv = tm // KV_TILE
    xb = x_ref[...].astype(BF16)
    h = jnp.dot(xb, win_ref[...], preferred_element_type=F32)
    cq = _rms_norm(h[:, :Q_RANK], qn_ref[...]).astype(BF16)
    ckv = _rms_norm(h[:, Q_RANK:Q_RANK + KV_RANK], kvn_ref[...]).astype(BF16)
    f_ref[...] = h[:, Q_RANK + KV_RANK:Q_RANK + KV_RANK + FWIDTH]
    kr = h[:, Q_RANK + KV_RANK + FWIDTH:]
    kr_rot = kr * cc_ref[...] + pltpu.roll(kr, ROPE, axis=1) * ss_ref[...]
    kr_rot = kr_rot[:, :ROPE]
    kr_sq = jnp.sum(kr_rot * kr_rot, axis=1, keepdims=True)
    kr_rot = kr_rot.astype(BF16)

    qT = lax.dot_general(wuqT_ref[...], cq, _NT, preferred_element_type=F32)
    cosT = cosT_ref[...]
    sinT = sinT_ref[...]
    half = ROPE // 2
    for hd in range(HEADS):
        base = hd * QK_DIM
        qT_ref[hd, 0:NOPE, :] = (qT[base:base + NOPE] * q_scale).astype(BF16)
        t1 = qT[base + NOPE:base + NOPE + half]
        t2 = qT[base + NOPE + half:base + QK_DIM]
        qT_ref[hd, NOPE:NOPE + half, :] = ((t1 * cosT - t2 * sinT) * q_scale).astype(BF16)
        qT_ref[hd, NOPE + half:QK_DIM, :] = ((t1 * sinT + t2 * cosT) * q_scale).astype(BF16)
        qh = qT[base:base + QK_DIM] * q_scale
        q_sq = jnp.max(jnp.sum(qh * qh, axis=0, keepdims=True), axis=1, keepdims=True)
        qmax_ref[0, hd:hd + 1, :] = jnp.broadcast_to(q_sq, (1, 128))

    kn = jnp.dot(ckv, wuk_ref[...], preferred_element_type=F32)
    vT = lax.dot_general(wuvT_ref[...], ckv, _NT, preferred_element_type=F32)
    for hd in range(HEADS):
        kh = kn[:, hd * NOPE:(hd + 1) * NOPE]
        k_sq = jnp.max(jnp.sum(kh * kh, axis=1, keepdims=True) + kr_sq, axis=0, keepdims=True)
        kmax_ref[0, hd:hd + 1, :] = jnp.broadcast_to(k_sq, (1, 128))
        for c in range(n_kv):
            rows = slice(c * KV_TILE, (c + 1) * KV_TILE)
            k_ref[hd, c, :, 0:NOPE] = kn[rows, hd * NOPE:(hd + 1) * NOPE].astype(BF16)
            k_ref[hd, c, :, NOPE:QK_DIM] = kr_rot[rows]
            vT_ref[hd, c] = vT[hd * V_DIM:(hd + 1) * V_DIM, rows].astype(BF16)


def _even_in(x, win, qn, wuqT, kvn, wuk, wuvT, cosT, sinT, cc, ss, i):
    S = x.shape[0]
    tm = ROW_TILE
    n_kv = tm // KV_TILE
    q_scale = (QK_DIM ** -0.5) * math.log2(math.e)
    row = lambda i: (i, 0)
    return pl.pallas_call(
        functools.partial(_even_in_kernel, q_scale=q_scale),
        grid=(S // tm,),
        in_specs=[
            pl.BlockSpec((tm, D_MODEL), row),
            _layer_of(win.shape, i), _layer_of(qn.shape, i), _layer_of(wuqT.shape, i),
            _layer_of(kvn.shape, i), _layer_of(wuk.shape, i), _layer_of(wuvT.shape, i),
            pl.BlockSpec((ROPE // 2, tm), lambda i: (0, i)),
            pl.BlockSpec((ROPE // 2, tm), lambda i: (0, i)),
            pl.BlockSpec((tm, 2 * ROPE), row),
            pl.BlockSpec((tm, 2 * ROPE), row),
        ],
        out_specs=[
            pl.BlockSpec((HEADS, QK_DIM, tm), lambda i: (0, 0, i)),
            pl.BlockSpec((HEADS, n_kv, KV_TILE, QK_DIM), lambda i: (0, i, 0, 0)),
            pl.BlockSpec((HEADS, n_kv, V_DIM, KV_TILE), lambda i: (0, i, 0, 0)),
            pl.BlockSpec((tm, FWIDTH), row),
            pl.BlockSpec((1, HEADS, 128), lambda i: (i, 0, 0)),
            pl.BlockSpec((1, HEADS, 128), lambda i: (i, 0, 0)),
        ],
        out_shape=[
            jax.ShapeDtypeStruct((HEADS, QK_DIM, S), BF16),
            jax.ShapeDtypeStruct((HEADS, S // KV_TILE, KV_TILE, QK_DIM), BF16),
            jax.ShapeDtypeStruct((HEADS, S // KV_TILE, V_DIM, KV_TILE), BF16),
            jax.ShapeDtypeStruct((S, FWIDTH), F32),
            jax.ShapeDtypeStruct((S // tm, HEADS, 128), F32),
            jax.ShapeDtypeStruct((S // tm, HEADS, 128), F32),
        ],
        compiler_params=_params(1),
        name="even_in",
    )(x, win, qn, wuqT, kvn, wuk, wuvT, cosT, sinT, cc, ss)


def _attn_bounded(qT_ref, k_ref, vT_ref, p0_ref, p1_ref, acc_ref, l_ref):
    n_kv, tk, _ = k_ref.shape
    tq = qT_ref.shape[1]
    qT = qT_ref[...]

    def probs(j, p_ref):
        s = jnp.dot(k_ref[j], qT, preferred_element_type=F32)
        p = jnp.exp2(s)
        p_ref[...] = p.astype(BF16)
        return jnp.sum(p.reshape(tk // 8, 8, tq), axis=0)

    def values(j, p_ref):
        acc_ref[...] += jnp.dot(vT_ref[j], p_ref[...], preferred_element_type=F32)

    acc_ref[...] = jnp.zeros_like(acc_ref)
    l8 = probs(0, p0_ref)

    def pair(i, l8):
        j = 2 * i
        l8 = l8 + probs(j + 1, p1_ref)
        values(j, p0_ref)
        l8 = l8 + probs(j + 2, p0_ref)
        values(j + 1, p1_ref)
        return l8

    l8 = lax.fori_loop(0, n_kv // 2 - 1, pair, l8, unroll=ATTN_UNROLL)
    l8 = l8 + probs(n_kv - 1, p1_ref)
    values(n_kv - 2, p0_ref)
    values(n_kv - 1, p1_ref)
    l_ref[...] = jnp.sum(l8, axis=0, keepdims=True)


def _attn_general(qT_ref, k_ref, vT_ref, s0_ref, s1_ref, p0_ref, p1_ref, acc_ref, l_ref):
    n_kv = k_ref.shape[0]
    tq = qT_ref.shape[1]
    qT = qT_ref[...]

    def scores(j, s_ref):
        s = jnp.dot(k_ref[j], qT, preferred_element_type=F32)
        s_ref[...] = s
        return jnp.max(s, axis=0, keepdims=True)

    def softmax(s_ref, p_ref, mx, m, l):
        m_new = jnp.maximum(m, mx)
        alpha = jnp.exp2(m - m_new)
        p = jnp.exp2(s_ref[...] - m_new)
        p_ref[...] = p.astype(BF16)
        return m_new, alpha * l + jnp.sum(p, axis=0, keepdims=True), alpha

    def values(j, p_ref, alpha):
        pv = jnp.dot(vT_ref[j], p_ref[...], preferred_element_type=F32)
        acc_ref[...] = alpha * acc_ref[...] + pv

    p1_ref[...] = jnp.zeros_like(p1_ref)
    acc_ref[...] = jnp.zeros_like(acc_ref)
    mx0 = scores(0, s0_ref)

    def pair(i, carry):
        mx0, m, l, alpha_prev = carry
        j = 2 * i
        mx1 = scores(j + 1, s1_ref)
        m, l, alpha0 = softmax(s0_ref, p0_ref, mx0, m, l)
        values(jnp.maximum(j - 1, 0), p1_ref, alpha_prev)
        mx0 = scores(jnp.minimum(j + 2, n_kv - 1), s0_ref)
        m, l, alpha1 = softmax(s1_ref, p1_ref, mx1, m, l)
        values(j, p0_ref, alpha0)
        return mx0, m, l, alpha1

    m0 = jnp.full((1, tq), -jnp.inf, F32)
    l0 = jnp.zeros((1, tq), F32)
    _, _, l, alpha = lax.fori_loop(0, n_kv // 2, pair, (mx0, m0, l0, jnp.ones((1, tq), F32)))
    values(n_kv - 1, p1_ref, alpha)
    l_ref[...] = l


def _attn_kernel(bounded_ref, qT_ref, k_ref, vT_ref, o_ref,
                 s0_ref, s1_ref, p0_ref, p1_ref, acc_ref, l_ref):
    bounded = bounded_ref[pl.program_id(0), pl.program_id(1)] != 0

    @pl.when(bounded)
    def _():
        _attn_bounded(qT_ref, k_ref, vT_ref, p0_ref, p1_ref, acc_ref, l_ref)

    @pl.when(jnp.logical_not(bounded))
    def _():
        _attn_general(qT_ref, k_ref, vT_ref, s0_ref, s1_ref, p0_ref, p1_ref, acc_ref, l_ref)

    o_ref[...] = (acc_ref[...] * (1.0 / l_ref[...])).T.astype(o_ref.dtype)


def _attention(bounded, qT, k, vT):
    H, n_kv, tk, _ = k.shape
    S = qT.shape[2]
    tq = Q_TILE
    assert n_kv % 2 == 0
    return pl.pallas_call(
        _attn_kernel,
        grid_spec=pltpu.PrefetchScalarGridSpec(
            num_scalar_prefetch=1,
            grid=(H, S // tq),
            in_specs=[
                pl.BlockSpec((None, QK_DIM, tq), lambda h, i, b: (h, 0, i)),
                pl.BlockSpec((None, n_kv, tk, QK_DIM), lambda h, i, b: (h, 0, 0, 0)),
                pl.BlockSpec((None, n_kv, V_DIM, tk), lambda h, i, b: (h, 0, 0, 0)),
            ],
            out_specs=pl.BlockSpec((tq, V_DIM), lambda h, i, b: (i, h)),
            scratch_shapes=[pltpu.VMEM((tk, tq), F32), pltpu.VMEM((tk, tq), F32),
                            pltpu.VMEM((tk, tq), BF16), pltpu.VMEM((tk, tq), BF16),
                            pltpu.VMEM((V_DIM, tq), F32), pltpu.VMEM((1, tq), F32)],
        ),
        out_shape=jax.ShapeDtypeStruct((S, H * V_DIM), BF16),
        compiler_params=_params(2),
        name="mla_attention",
    )(bounded, qT, k, vT)


def _score_bound_flags(qmax_sq, kmax_sq):
    n_blk = qmax_sq.shape[0]
    per = Q_TILE // ROW_TILE
    q_sq = jnp.max(qmax_sq[:, :, 0].reshape(n_blk // per, per, HEADS), axis=1)
    k_sq = jnp.max(kmax_sq[:, :, 0], axis=0)
    ok = q_sq * k_sq[None, :] * (BOUND_MARGIN ** 2) <= SCORE_BOUND_LOG2 ** 2
    return ok.astype(jnp.int32).T


def _dft_consts():
    n = np.arange(DFT_N)
    ang = 2.0 * np.pi * np.outer(n, n) / DFT_N
    c = np.cos(ang) / np.sqrt(DFT_N)
    s = np.sin(ang) / np.sqrt(DFT_N)
    w_a = np.concatenate([c, -s], axis=0)
    w_b = np.block([[c, s], [-s, c]])
    w_c = np.concatenate([c, s], axis=0)
    return tuple(jnp.asarray(w, F32).astype(BF16) for w in (w_a, w_b, w_c))


def _dft_a_kernel(x_ref, w_ref, tr_ref, ti_ref, xs_ref, ts_ref):
    w = w_ref[...]
    for r in range(DFT_ROWS):
        xs_ref[r] = x_ref[:, r, :]
    for r in range(DFT_ROWS):
        ts_ref[r] = jnp.dot(w, xs_ref[r].astype(BF16), preferred_element_type=F32)
    for r in range(DFT_ROWS):
        tr_ref[:, r, :] = ts_ref[r, :DFT_N, :]
        ti_ref[:, r, :] = ts_ref[r, DFT_N:, :]


def _dft_b_kernel(tr_ref, ti_ref, twc_ref, tws_ref, wb_ref, wc_ref, y_ref, ys_ref):
    re, im = [], []
    for j in range(DFT_ROWS):
        tr, ti = tr_ref[j], ti_ref[j]
        cw = jnp.concatenate([twc_ref[j]] * FGROUPS, axis=1)
        sw = jnp.concatenate([tws_ref[j]] * FGROUPS, axis=1)
        re.append((tr * cw + ti * sw).astype(BF16))
        im.append((ti * cw - tr * sw).astype(BF16))
    st = jnp.concatenate([jnp.concatenate(re, axis=1), jnp.concatenate(im, axis=1)], axis=0)
    g = jnp.dot(wb_ref[...], st, preferred_element_type=F32).astype(BF16)
    n_blk = DFT_ROWS * FGROUPS
    gg = jnp.concatenate(
        [jnp.concatenate([g[:DFT_N, b * FDIM:(b + 1) * FDIM], g[DFT_N:, b * FDIM:(b + 1) * FDIM]], axis=1)
         for b in range(n_blk)], axis=0)
    y = jnp.dot(gg, wc_ref[...], preferred_element_type=F32)
    for j in range(DFT_ROWS):
        for grp in range(FGROUPS):
            b = j * FGROUPS + grp
            ys_ref[j, :, grp * FDIM:(grp + 1) * FDIM] = y[b * DFT_N:(b + 1) * DFT_N]
    for j in range(DFT_ROWS):
        y_ref[:, j, :] = ys_ref[j]


def _fourier_mix(f, twc, tws):
    S = f.shape[0]
    w_a, w_b, w_c = _dft_consts()
    cube = (DFT_N, DFT_N, FWIDTH)
    steps = (DFT_N // DFT_ROWS,)
    inner = pl.BlockSpec((DFT_N, DFT_ROWS, FWIDTH), lambda i: (0, i, 0))
    outer = pl.BlockSpec((DFT_ROWS, DFT_N, FWIDTH), lambda i: (i, 0, 0))
    tr, ti = pl.pallas_call(
        _dft_a_kernel,
        grid=steps,
        in_specs=[inner, _resident(w_a.shape)],
        out_specs=[inner, inner],
        out_shape=[jax.ShapeDtypeStruct(cube, F32)] * 2,
        scratch_shapes=[pltpu.VMEM((DFT_ROWS, DFT_N, FWIDTH), F32),
                        pltpu.VMEM((DFT_ROWS, 2 * DFT_N, FWIDTH), F32)],
        compiler_params=_params(1),
        name="dft_positions_outer",
    )(f.reshape(cube), w_a)
    y = pl.pallas_call(
        _dft_b_kernel,
        grid=steps,
        in_specs=[outer, outer,
                  pl.BlockSpec((DFT_ROWS, DFT_N, FDIM), lambda i: (i, 0, 0)),
                  pl.BlockSpec((DFT_ROWS, DFT_N, FDIM), lambda i: (i, 0, 0)),
                  _resident(w_b.shape), _resident(w_c.shape)],
        out_specs=inner,
        out_shape=jax.ShapeDtypeStruct(cube, F32),
        scratch_shapes=[pltpu.VMEM((DFT_ROWS, DFT_N, FWIDTH), F32)],
        compiler_params=_params(1),
        name="dft_positions_inner_channels",
    )(tr, ti, twc, tws, w_b, w_c)
    return y.reshape(S, FWIDTH)


def _even_out_ffn_kernel(a_ref, f_ref, x_ref, wa_ref, wf_ref, g_ref, b_ref,
                         wg_ref, wu_ref, wd_ref, fg_ref, fb_ref, o_ref):
    y = jnp.dot(a_ref[...], wa_ref[...], preferred_element_type=F32)
    y = y + jnp.dot(f_ref[...].astype(BF16), wf_ref[...], preferred_element_type=F32)
    h = _layer_norm(DN_ALPHA * x_ref[...] + y, g_ref[...], b_ref[...])
    o_ref[...] = _swiglu_ln(h, wg_ref, wu_ref, wd_ref, fg_ref, fb_ref)


def _even_out_ffn(attn, four, x, wo, g, b, wg, wu, wd, fg, fb, i, layer):
    S = x.shape[0]
    tm = ROW_TILE
    row = lambda r: (r, 0)
    n_attn = HEADS * V_DIM
    wa = pl.BlockSpec((None, n_attn, D_MODEL), lambda *_: (i, 0, 0), pipeline_mode=pl.Buffered(1))
    wf = pl.BlockSpec((None, FWIDTH, D_MODEL), lambda *_: (i, n_attn // FWIDTH, 0),
                      pipeline_mode=pl.Buffered(1))
    return pl.pallas_call(
        _even_out_ffn_kernel,
        grid=(S // tm,),
        in_specs=[pl.BlockSpec((tm, n_attn), row), pl.BlockSpec((tm, FWIDTH), row),
                  pl.BlockSpec((tm, D_MODEL), row),
                  wa, wf, _layer_of(g.shape, layer), _layer_of(b.shape, layer),
                  _layer_of(wg.shape, layer), _layer_of(wu.shape, layer), _layer_of(wd.shape, layer),
                  _layer_of(fg.shape, layer), _layer_of(fb.shape, layer)],
        out_specs=pl.BlockSpec((tm, D_MODEL), row),
        out_shape=jax.ShapeDtypeStruct((S, D_MODEL), F32),
        compiler_params=_params(1),
        name="even_out_ln_swiglu_ln",
    )(attn, four, x, wo, wo, g, b, wg, wu, wd, fg, fb)


def _odd_kernel(x_ref, win_ref, ng_ref, nb_ref, ws_ref, bs_ref, wout_ref, g_ref, b_ref, o_ref):
    tm = x_ref.shape[0]
    x = x_ref[...]
    z = jnp.dot(x.astype(BF16), win_ref[...], preferred_element_type=F32)
    z = 0.5 * z * (1.0 + lax.erf(z * (2.0 ** -0.5)))
    u = z[:, :SGU_WIDTH]
    v = _layer_norm(z[:, SGU_WIDTH:], ng_ref[...], nb_ref[...]).astype(BF16)
    gated = []
    for c in range(tm // SGU_CHUNK):
        rows = slice(c * SGU_CHUNK, (c + 1) * SGU_CHUNK)
        parts = []
        for grp in range(SGU_GROUPS):
            cols = slice(grp * SGU_GDIM, (grp + 1) * SGU_GDIM)
            s = jnp.dot(ws_ref[grp], v[rows, cols], preferred_element_type=F32) + bs_ref[grp]
            parts.append((u[rows, cols] * s).astype(BF16))
        gated.append(jnp.concatenate(parts, axis=1))
    gated = jnp.concatenate(gated, axis=0)
    y = jnp.dot(gated, wout_ref[...], preferred_element_type=F32)
    o_ref[...] = _layer_norm(DN_ALPHA * x + y, g_ref[...], b_ref[...])


def _odd_layer(x, win, ng, nb, ws, bs, wout, g, b, i, layer):
    S = x.shape[0]
    tm = SGU_ROWS
    row = lambda r: (r, 0)
    return pl.pallas_call(
        _odd_kernel,
        grid=(S // tm,),
        in_specs=[pl.BlockSpec((tm, D_MODEL), row),
                  _layer_of(win.shape, i), _layer_of(ng.shape, i), _layer_of(nb.shape, i),
                  _layer_of(ws.shape, i), _layer_of(bs.shape, i), _layer_of(wout.shape, i),
                  _layer_of(g.shape, layer), _layer_of(b.shape, layer)],
        out_specs=pl.BlockSpec((tm, D_MODEL), row),
        out_shape=jax.ShapeDtypeStruct((S, D_MODEL), F32),
        compiler_params=_params(1),
        name="odd_sgu_ln",
    )(x, win, ng, nb, ws, bs, wout, g, b)


def _swiglu_ln(x, wg_ref, wu_ref, wd_ref, g_ref, b_ref):
    xb = x.astype(BF16)
    y = None
    start = 0
    for width in FF_CHUNKS:
        cols = slice(start, start + width)
        start += width
        gate = jnp.dot(xb, wg_ref[:, cols], preferred_element_type=F32)
        up = jnp.dot(xb, wu_ref[:, cols], preferred_element_type=F32)
        hid = (jax.nn.silu(gate) * up).astype(BF16)
        part = jnp.dot(hid, wd_ref[cols, :], preferred_element_type=F32)
        y = part if y is None else y + part
    return _layer_norm(DN_ALPHA * x + y, g_ref[...], b_ref[...])


def _ffn_kernel(x_ref, wg_ref, wu_ref, wd_ref, g_ref, b_ref, o_ref):
    o_ref[...] = _swiglu_ln(x_ref[...], wg_ref, wu_ref, wd_ref, g_ref, b_ref)


def _ffn(x, wg, wu, wd, g, b, layer):
    S = x.shape[0]
    tm = ROW_TILE
    row = lambda r: (r, 0)
    return pl.pallas_call(
        _ffn_kernel,
        grid=(S // tm,),
        in_specs=[pl.BlockSpec((tm, D_MODEL), row),
                  _layer_of(wg.shape, layer), _layer_of(wu.shape, layer), _layer_of(wd.shape, layer),
                  _layer_of(g.shape, layer), _layer_of(b.shape, layer)],
        out_specs=pl.BlockSpec((tm, D_MODEL), row),
        out_shape=jax.ShapeDtypeStruct((S, D_MODEL), F32),
        compiler_params=_params(1),
        name="swiglu_ln",
    )(x, wg, wu, wd, g, b)


def _rotary_tables(seq):
    inv = 1.0 / (ROPE_THETA ** (jnp.arange(0, ROPE, 2, dtype=F32) / ROPE))
    ang = jnp.arange(seq, dtype=F32)[:, None] * inv[None, :]
    cos, sin = lax.optimization_barrier((jnp.cos(ang), jnp.sin(ang)))
    eye = np.eye(ROPE // 2, dtype=np.float32)
    rep_c = jnp.asarray(np.concatenate([eye, eye, eye, eye], axis=1))
    rep_s = jnp.asarray(np.concatenate([-eye, eye, -eye, eye], axis=1))
    cc = jnp.dot(cos, rep_c, precision=lax.Precision.HIGHEST)
    ss = jnp.dot(sin, rep_s, precision=lax.Precision.HIGHEST)
    return cos.T, sin.T, cc, ss


def _twiddle_tables():
    k2 = jnp.arange(DFT_N, dtype=jnp.int32)[:, None]
    n1 = jnp.arange(DFT_N, dtype=jnp.int32)[None, :]
    ang = (k2 * n1).astype(F32) * F32(2.0 * np.pi / SEQ)
    ang = ang.reshape(DFT_N, DFT_N, 1)
    cos, sin = lax.optimization_barrier((jnp.cos(ang), jnp.sin(ang)))
    return (jnp.broadcast_to(cos, (DFT_N, DFT_N, FDIM)),
            jnp.broadcast_to(sin, (DFT_N, DFT_N, FDIM)))


def kernel(x, even_w_in, even_q_norm, even_w_uq, even_kv_norm, even_w_uk, even_w_uv, even_w_out,
           odd_w_in, odd_sgu_norm_g, odd_sgu_norm_b, odd_w_spatial, odd_b_spatial, odd_w_out,
           mix_ln_g, mix_ln_b, ffn_w_gate, ffn_w_up, ffn_w_down, ffn_ln_g, ffn_ln_b):
    B, S, D = x.shape
    assert (B, S, D) == (1, SEQ, D_MODEL)
    cosT, sinT, cc, ss = _rotary_tables(S)
    twc, tws = _twiddle_tables()
    half = ROPE // 2
    c0 = Q_RANK + KV_RANK
    c1 = c0 + ROPE
    w = even_w_in
    even_win = jnp.concatenate(
        [w[..., :c0], w[..., c1:], w[..., c0:c1], w[..., c0 + half:c1], w[..., c0:c0 + half]],
        axis=-1).astype(BF16)
    even_qn = even_q_norm[:, None, :]
    even_kvn = even_kv_norm[:, None, :]
    even_wuqT = jnp.swapaxes(even_w_uq, 1, 2).astype(BF16)
    even_wuk = even_w_uk.astype(BF16)
    even_wuvT = jnp.swapaxes(even_w_uv, 1, 2).astype(BF16)
    even_wo = even_w_out.astype(BF16)
    odd_win = odd_w_in.astype(BF16)
    odd_ng = odd_sgu_norm_g[:, None, :]
    odd_nb = odd_sgu_norm_b[:, None, :]
    odd_ws = odd_w_spatial.astype(BF16)
    odd_bs = odd_b_spatial[..., None]
    odd_wout = odd_w_out.astype(BF16)
    mix_g, mix_b = mix_ln_g[:, None, :], mix_ln_b[:, None, :]
    ffn_g, ffn_b = ffn_ln_g[:, None, :], ffn_ln_b[:, None, :]
    ffn_wg, ffn_wu, ffn_wd = ffn_w_gate.astype(BF16), ffn_w_up.astype(BF16), ffn_w_down.astype(BF16)

    h = x.reshape(S, D)
    for layer in range(DEPTH):
        i = layer // 2
        if layer % 2 == 0:
            qT, k, vT, f, qmax_sq, kmax_sq = _even_in(
                h, even_win, even_qn, even_wuqT, even_kvn, even_wuk, even_wuvT, cosT, sinT, cc, ss, i)
            attn = _attention(_score_bound_flags(qmax_sq, kmax_sq), qT, k, vT)
            four = _fourier_mix(f, twc, tws)
            h = _even_out_ffn(attn, four, h, even_wo, mix_g, mix_b,
                              ffn_wg, ffn_wu, ffn_wd, ffn_g, ffn_b, i, layer)
        else:
            h = _odd_layer(h, odd_win, odd_ng, odd_nb, odd_ws, odd_bs, odd_wout, mix_g, mix_b, i, layer)
            h = _ffn(h, ffn_wg, ffn_wu, ffn_wd, ffn_g, ffn_b, layer)
    return h.reshape(B, S, D)
```

```python
import functools
import math

import numpy as np
import jax
import jax.numpy as jnp
from jax import lax
from jax.experimental import pallas as pl
from jax.experimental.pallas import tpu as pltpu

F32 = jnp.float32
BF16 = jnp.bfloat16

D_MODEL = 1024
SEQ = 16384
DEPTH = 4
HEADS = 8
NOPE = 128
ROPE = 64
QK_DIM = NOPE + ROPE
V_DIM = 128
Q_RANK = 384
KV_RANK = 256
ROPE_THETA = 10000.0
FGROUPS = 4
FDIM = 128
FWIDTH = FGROUPS * FDIM
SGU_CHUNK = 128
SGU_GROUPS = 8
SGU_WIDTH = 2 * D_MODEL
SGU_GDIM = SGU_WIDTH // SGU_GROUPS
D_FF = 2816
DN_ALPHA = (2 * DEPTH) ** 0.25
LN_EPS = 1e-5
RMS_EPS = 1e-6

VMEM_LIMIT_BYTES = 56 * 1024 * 1024
DFT_N = 128
DFT_ROWS = 8

ROW_TILE = 512
KV_TILE = 512
Q_TILE = 4096
GENERAL_Q = 2048
FF_CHUNKS = (1536, 1280)
SGU_ROWS = 512
ATTN_UNROLL = 5
SCORE_BOUND_LOG2 = 60.0
BOUND_MARGIN = 1.05

_NT = (((1,), (1,)), ((), ()))


def _params(n_axes):
    return pltpu.CompilerParams(
        dimension_semantics=("arbitrary",) * n_axes, vmem_limit_bytes=VMEM_LIMIT_BYTES)


def _resident(shape):
    nd = len(shape)
    return pl.BlockSpec(shape, lambda *_: (0,) * nd, pipeline_mode=pl.Buffered(1))


def _layer_of(stacked_shape, layer):
    nd = len(stacked_shape)
    return pl.BlockSpec((None,) + tuple(stacked_shape[1:]), lambda *_: (layer,) + (0,) * (nd - 1),
                        pipeline_mode=pl.Buffered(1))


def _layer_norm(z, g, b):
    mu = jnp.mean(z, axis=-1, keepdims=True)
    zc = z - mu
    var = jnp.mean(zc * zc, axis=-1, keepdims=True)
    return zc * lax.rsqrt(var + LN_EPS) * g + b


def _rms_norm(z, g):
    ms = jnp.mean(z * z, axis=-1, keepdims=True)
    return z * lax.rsqrt(ms + RMS_EPS) * g


def _even_in_kernel(x_ref, win_ref, qn_ref, wuqT_ref, kvn_ref, wuk_ref, wuvT_ref,
                    cosT_ref, sinT_ref, cc_ref, ss_ref,
                    qT_ref, k_ref, vT_ref, f_ref, qmax_ref, kmax_ref, *, q_scale):
    tm = x_ref.shape[0]
    n_kv = tm // KV_TILE
    xb = x_ref[...].astype(BF16)
    h = jnp.dot(xb, win_ref[...], preferred_element_type=F32)
    cq = _rms_norm(h[:, :Q_RANK], qn_ref[...]).astype(BF16)
    ckv = _rms_norm(h[:, Q_RANK:Q_RANK + KV_RANK], kvn_ref[...]).astype(BF16)
    f_ref[...] = h[:, Q_RANK + KV_RANK:Q_RANK + KV_RANK + FWIDTH]
    kr = h[:, Q_RANK + KV_RANK + FWIDTH:]
    kr_rot = kr * cc_ref[...] + pltpu.roll(kr, ROPE, axis=1) * ss_ref[...]
    kr_rot = kr_rot[:, :ROPE]
    kr_sq = jnp.sum(kr_rot * kr_rot, axis=1, keepdims=True)
    kr_rot = kr_rot.astype(BF16)

    qT = lax.dot_general(wuqT_ref[...], cq, _NT, preferred_element_type=F32)
    cosT = cosT_ref[...]
    sinT = sinT_ref[...]
    half = ROPE // 2
    for hd in range(HEADS):
        base = hd * QK_DIM
        qT_ref[hd, 0:NOPE, :] = (qT[base:base + NOPE] * q_scale).astype(BF16)
        t1 = qT[base + NOPE:base + NOPE + half]
        t2 = qT[base + NOPE + half:base + QK_DIM]
        qT_ref[hd, NOPE:NOPE + half, :] = ((t1 * cosT - t2 * sinT) * q_scale).astype(BF16)
        qT_ref[hd, NOPE + half:QK_DIM, :] = ((t1 * sinT + t2 * cosT) * q_scale).astype(BF16)
        qh = qT[base:base + QK_DIM] * q_scale
        q_sq = jnp.max(jnp.sum(qh * qh, axis=0, keepdims=True), axis=1, keepdims=True)
        qmax_ref[0, hd:hd + 1, :] = jnp.broadcast_to(q_sq, (1, 128))

    kn = jnp.dot(ckv, wuk_ref[...], preferred_element_type=F32)
    vT = lax.dot_general(wuvT_ref[...], ckv, _NT, preferred_element_type=F32)
    for hd in range(HEADS):
        kh = kn[:, hd * NOPE:(hd + 1) * NOPE]
        k_sq = jnp.max(jnp.sum(kh * kh, axis=1, keepdims=True) + kr_sq, axis=0, keepdims=True)
        kmax_ref[0, hd:hd + 1, :] = jnp.broadcast_to(k_sq, (1, 128))
        for c in range(n_kv):
            rows = slice(c * KV_TILE, (c + 1) * KV_TILE)
            k_ref[hd, c, :, 0:NOPE] = kn[rows, hd * NOPE:(hd + 1) * NOPE].astype(BF16)
            k_ref[hd, c, :, NOPE:QK_DIM] = kr_rot[rows]
            vT_ref[hd, c] = vT[hd * V_DIM:(hd + 1) * V_DIM, rows].astype(BF16)


def _even_in(x, win, qn, wuqT, kvn, wuk, wuvT, cosT, sinT, cc, ss, i):
    S = x.shape[0]
    tm = ROW_TILE
    n_kv = tm // KV_TILE
    q_scale = (QK_DIM ** -0.5) * math.log2(math.e)
    row = lambda i: (i, 0)
    return pl.pallas_call(
        functools.partial(_even_in_kernel, q_scale=q_scale),
        grid=(S // tm,),
        in_specs=[
            pl.BlockSpec((tm, D_MODEL), row),
            _layer_of(win.shape, i), _layer_of(qn.shape, i), _layer_of(wuqT.shape, i),
            _layer_of(kvn.shape, i), _layer_of(wuk.shape, i), _layer_of(wuvT.shape, i),
            pl.BlockSpec((ROPE // 2, tm), lambda i: (0, i)),
            pl.BlockSpec((ROPE // 2, tm), lambda i: (0, i)),
            pl.BlockSpec((tm, 2 * ROPE), row),
            pl.BlockSpec((tm, 2 * ROPE), row),
        ],
        out_specs=[
            pl.BlockSpec((HEADS, QK_DIM, tm), lambda i: (0, 0, i)),
            pl.BlockSpec((HEADS, n_kv, KV_TILE, QK_DIM), lambda i: (0, i, 0, 0)),
            pl.BlockSpec((HEADS, n_kv, V_DIM, KV_TILE), lambda i: (0, i, 0, 0)),
            pl.BlockSpec((tm, FWIDTH), row),
            pl.BlockSpec((1, HEADS, 128), lambda i: (i, 0, 0)),
            pl.BlockSpec((1, HEADS, 128), lambda i: (i, 0, 0)),
        ],
        out_shape=[
            jax.ShapeDtypeStruct((HEADS, QK_DIM, S), BF16),
            jax.ShapeDtypeStruct((HEADS, S // KV_TILE, KV_TILE, QK_DIM), BF16),
            jax.ShapeDtypeStruct((HEADS, S // KV_TILE, V_DIM, KV_TILE), BF16),
            jax.ShapeDtypeStruct((S, FWIDTH), F32),
            jax.ShapeDtypeStruct((S // tm, HEADS, 128), F32),
            jax.ShapeDtypeStruct((S // tm, HEADS, 128), F32),
        ],
        compiler_params=_params(1),
        name="even_in",
    )(x, win, qn, wuqT, kvn, wuk, wuvT, cosT, sinT, cc, ss)


def _attn_bounded(qT_ref, k_ref, vT_ref, p0_ref, p1_ref, acc_ref, l_ref):
    n_kv, tk, _ = k_ref.shape
    tq = qT_ref.shape[1]
    qT = qT_ref[...]

    def probs(j, p_ref):
        s = jnp.dot(k_ref[j], qT, preferred_element_type=F32)
        p = jnp.exp2(s)
        p_ref[...] = p.astype(BF16)
        return jnp.sum(p.reshape(tk // 8, 8, tq), axis=0)

    def values(j, p_ref):
        acc_ref[...] += jnp.dot(vT_ref[j], p_ref[...], preferred_element_type=F32)

    acc_ref[...] = jnp.zeros_like(acc_ref)
    l8 = probs(0, p0_ref)

    def pair(i, l8):
        j = 2 * i
        l8 = l8 + probs(j + 1, p1_ref)
        values(j, p0_ref)
        l8 = l8 + probs(j + 2, p0_ref)
        values(j + 1, p1_ref)
        return l8

    l8 = lax.fori_loop(0, n_kv // 2 - 1, pair, l8, unroll=ATTN_UNROLL)
    l8 = l8 + probs(n_kv - 1, p1_ref)
    values(n_kv - 2, p0_ref)
    values(n_kv - 1, p1_ref)
    l_ref[...] = jnp.sum(l8, axis=0, keepdims=True)


def _attn_general(qT_ref, k_ref, vT_ref, s0_ref, s1_ref, p0_ref, p1_ref, acc_ref, l_ref):
    n_kv = k_ref.shape[0]
    tq = GENERAL_Q
    for seg in range(qT_ref.shape[1] // tq):
        cols = slice(seg * tq, (seg + 1) * tq)
        qT = qT_ref[:, cols]

        def scores(j, s_ref):
            s = jnp.dot(k_ref[j], qT, preferred_element_type=F32)
            s_ref[...] = s
            return jnp.max(s, axis=0, keepdims=True)

        def softmax(s_ref, p_ref, mx, m, l):
            m_new = jnp.maximum(m, mx)
            alpha = jnp.exp2(m - m_new)
            p = jnp.exp2(s_ref[...] - m_new)
            p_ref[:, :tq] = p.astype(BF16)
            return m_new, alpha * l + jnp.sum(p, axis=0, keepdims=True), alpha

        def values(j, p_ref, alpha):
            pv = jnp.dot(vT_ref[j], p_ref[:, :tq], preferred_element_type=F32)
            acc_ref[:, cols] = alpha * acc_ref[:, cols] + pv

        p1_ref[:, :tq] = jnp.zeros((p1_ref.shape[0], tq), BF16)
        acc_ref[:, cols] = jnp.zeros((acc_ref.shape[0], tq), F32)
        mx0 = scores(0, s0_ref)

        def pair(i, carry):
            mx0, m, l, alpha_prev = carry
            j = 2 * i
            mx1 = scores(j + 1, s1_ref)
            m, l, alpha0 = softmax(s0_ref, p0_ref, mx0, m, l)
            values(jnp.maximum(j - 1, 0), p1_ref, alpha_prev)
            mx0 = scores(jnp.minimum(j + 2, n_kv - 1), s0_ref)
            m, l, alpha1 = softmax(s1_ref, p1_ref, mx1, m, l)
            values(j, p0_ref, alpha0)
            return mx0, m, l, alpha1

        m0 = jnp.full((1, tq), -jnp.inf, F32)
        l0 = jnp.zeros((1, tq), F32)
        _, _, l, alpha = lax.fori_loop(0, n_kv // 2, pair, (mx0, m0, l0, jnp.ones((1, tq), F32)))
        values(n_kv - 1, p1_ref, alpha)
        l_ref[:, cols] = l


def _attn_kernel(bounded_ref, qT_ref, k_ref, vT_ref, o_ref,
                 s0_ref, s1_ref, p0_ref, p1_ref, acc_ref, l_ref):
    bounded = bounded_ref[pl.program_id(0), pl.program_id(1)] != 0

    @pl.when(bounded)
    def _():
        _attn_bounded(qT_ref, k_ref, vT_ref, p0_ref, p1_ref, acc_ref, l_ref)

    @pl.when(jnp.logical_not(bounded))
    def _():
        _attn_general(qT_ref, k_ref, vT_ref, s0_ref, s1_ref, p0_ref, p1_ref, acc_ref, l_ref)

    o_ref[...] = (acc_ref[...] * (1.0 / l_ref[...])).T.astype(o_ref.dtype)


def _attention(bounded, qT, k, vT):
    H, n_kv, tk, _ = k.shape
    S = qT.shape[2]
    tq = Q_TILE
    assert n_kv % 2 == 0
    return pl.pallas_call(
        _attn_kernel,
        grid_spec=pltpu.PrefetchScalarGridSpec(
            num_scalar_prefetch=1,
            grid=(H, S // tq),
            in_specs=[
                pl.BlockSpec((None, QK_DIM, tq), lambda h, i, b: (h, 0, i)),
                pl.BlockSpec((None, n_kv, tk, QK_DIM), lambda h, i, b: (h, 0, 0, 0)),
                pl.BlockSpec((None, n_kv, V_DIM, tk), lambda h, i, b: (h, 0, 0, 0)),
            ],
            out_specs=pl.BlockSpec((tq, V_DIM), lambda h, i, b: (i, h)),
            scratch_shapes=[pltpu.VMEM((tk, GENERAL_Q), F32), pltpu.VMEM((tk, GENERAL_Q), F32),
                            pltpu.VMEM((tk, tq), BF16), pltpu.VMEM((tk, tq), BF16),
                            pltpu.VMEM((V_DIM, tq), F32), pltpu.VMEM((1, tq), F32)],
        ),
        out_shape=jax.ShapeDtypeStruct((S, H * V_DIM), BF16),
        compiler_params=_params(2),
        name="mla_attention",
    )(bounded, qT, k, vT)


def _score_bound_flags(qmax_sq, kmax_sq):
    n_blk = qmax_sq.shape[0]
    per = Q_TILE // ROW_TILE
    q_sq = jnp.max(qmax_sq[:, :, 0].reshape(n_blk // per, per, HEADS), axis=1)
    k_sq = jnp.max(kmax_sq[:, :, 0], axis=0)
    ok = q_sq * k_sq[None, :] * (BOUND_MARGIN ** 2) <= SCORE_BOUND_LOG2 ** 2
    return ok.astype(jnp.int32).T


def _dft_consts():
    n = np.arange(DFT_N)
    ang = 2.0 * np.pi * np.outer(n, n) / DFT_N
    c = np.cos(ang) / np.sqrt(DFT_N)
    s = np.sin(ang) / np.sqrt(DFT_N)
    w_a = np.concatenate([c, -s], axis=0)
    w_b = np.block([[c, s], [-s, c]])
    w_c = np.concatenate([c, s], axis=0)
    return tuple(jnp.asarray(w, F32).astype(BF16) for w in (w_a, w_b, w_c))


def _dft_a_kernel(x_ref, w_ref, tr_ref, ti_ref, xs_ref, ts_ref):
    w = w_ref[...]
    for r in range(DFT_ROWS):
        xs_ref[r] = x_ref[:, r, :]
    for r in range(DFT_ROWS):
        ts_ref[r] = jnp.dot(w, xs_ref[r].astype(BF16), preferred_element_type=F32)
    for r in range(DFT_ROWS):
        tr_ref[:, r, :] = ts_ref[r, :DFT_N, :]
        ti_ref[:, r, :] = ts_ref[r, DFT_N:, :]


def _dft_b_kernel(tr_ref, ti_ref, twc_ref, tws_ref, wb_ref, wc_ref, y_ref, ys_ref):
    re, im = [], []
    for j in range(DFT_ROWS):
        tr, ti = tr_ref[j], ti_ref[j]
        cw = jnp.concatenate([twc_ref[j]] * FGROUPS, axis=1)
        sw = jnp.concatenate([tws_ref[j]] * FGROUPS, axis=1)
        re.append((tr * cw + ti * sw).astype(BF16))
        im.append((ti * cw - tr * sw).astype(BF16))
    st = jnp.concatenate([jnp.concatenate(re, axis=1), jnp.concatenate(im, axis=1)], axis=0)
    g = jnp.dot(wb_ref[...], st, preferred_element_type=F32).astype(BF16)
    n_blk = DFT_ROWS * FGROUPS
    gg = jnp.concatenate(
        [jnp.concatenate([g[:DFT_N, b * FDIM:(b + 1) * FDIM], g[DFT_N:, b * FDIM:(b + 1) * FDIM]], axis=1)
         for b in range(n_blk)], axis=0)
    y = jnp.dot(gg, wc_ref[...], preferred_element_type=F32)
    for j in range(DFT_ROWS):
        for grp in range(FGROUPS):
            b = j * FGROUPS + grp
            ys_ref[j, :, grp * FDIM:(grp + 1) * FDIM] = y[b * DFT_N:(b + 1) * DFT_N]
    for j in range(DFT_ROWS):
        y_ref[:, j, :] = ys_ref[j]


def _fourier_mix(f, twc, tws):
    S = f.shape[0]
    w_a, w_b, w_c = _dft_consts()
    cube = (DFT_N, DFT_N, FWIDTH)
    steps = (DFT_N // DFT_ROWS,)
    inner = pl.BlockSpec((DFT_N, DFT_ROWS, FWIDTH), lambda i: (0, i, 0))
    outer = pl.BlockSpec((DFT_ROWS, DFT_N, FWIDTH), lambda i: (i, 0, 0))
    tr, ti = pl.pallas_call(
        _dft_a_kernel,
        grid=steps,
        in_specs=[inner, _resident(w_a.shape)],
        out_specs=[inner, inner],
        out_shape=[jax.ShapeDtypeStruct(cube, F32)] * 2,
        scratch_shapes=[pltpu.VMEM((DFT_ROWS, DFT_N, FWIDTH), F32),
                        pltpu.VMEM((DFT_ROWS, 2 * DFT_N, FWIDTH), F32)],
        compiler_params=_params(1),
        name="dft_positions_outer",
    )(f.reshape(cube), w_a)
    y = pl.pallas_call(
        _dft_b_kernel,
        grid=steps,
        in_specs=[outer, outer,
                  pl.BlockSpec((DFT_ROWS, DFT_N, FDIM), lambda i: (i, 0, 0)),
                  pl.BlockSpec((DFT_ROWS, DFT_N, FDIM), lambda i: (i, 0, 0)),
                  _resident(w_b.shape), _resident(w_c.shape)],
        out_specs=inner,
        out_shape=jax.ShapeDtypeStruct(cube, F32),
        scratch_shapes=[pltpu.VMEM((DFT_ROWS, DFT_N, FWIDTH), F32)],
        compiler_params=_params(1),
        name="dft_positions_inner_channels",
    )(tr, ti, twc, tws, w_b, w_c)
    return y.reshape(S, FWIDTH)


def _even_out_ffn_kernel(a_ref, f_ref, x_ref, wa_ref, wf_ref, g_ref, b_ref,
                         wg_ref, wu_ref, wd_ref, fg_ref, fb_ref, o_ref):
    y = jnp.dot(a_ref[...], wa_ref[...], preferred_element_type=F32)
    y = y + jnp.dot(f_ref[...].astype(BF16), wf_ref[...], preferred_element_type=F32)
    h = _layer_norm(DN_ALPHA * x_ref[...] + y, g_ref[...], b_ref[...])
    o_ref[...] = _swiglu_ln(h, wg_ref, wu_ref, wd_ref, fg_ref, fb_ref)


def _even_out_ffn(attn, four, x, wo, g, b, wg, wu, wd, fg, fb, i, layer):
    S = x.shape[0]
    tm = ROW_TILE
    row = lambda r: (r, 0)
    n_attn = HEADS * V_DIM
    wa = pl.BlockSpec((None, n_attn, D_MODEL), lambda *_: (i, 0, 0), pipeline_mode=pl.Buffered(1))
    wf = pl.BlockSpec((None, FWIDTH, D_MODEL), lambda *_: (i, n_attn // FWIDTH, 0),
                      pipeline_mode=pl.Buffered(1))
    return pl.pallas_call(
        _even_out_ffn_kernel,
        grid=(S // tm,),
        in_specs=[pl.BlockSpec((tm, n_attn), row), pl.BlockSpec((tm, FWIDTH), row),
                  pl.BlockSpec((tm, D_MODEL), row),
                  wa, wf, _layer_of(g.shape, layer), _layer_of(b.shape, layer),
                  _layer_of(wg.shape, layer), _layer_of(wu.shape, layer), _layer_of(wd.shape, layer),
                  _layer_of(fg.shape, layer), _layer_of(fb.shape, layer)],
        out_specs=pl.BlockSpec((tm, D_MODEL), row),
        out_shape=jax.ShapeDtypeStruct((S, D_MODEL), F32),
        compiler_params=_params(1),
        name="even_out_ln_swiglu_ln",
    )(attn, four, x, wo, wo, g, b, wg, wu, wd, fg, fb)


def _odd_kernel(x_ref, win_ref, ng_ref, nb_ref, ws_ref, bs_ref, wout_ref, g_ref, b_ref,
                wg_ref, wu_ref, wd_ref, fg_ref, fb_ref, o_ref):
    tm = x_ref.shape[0]
    x = x_ref[...]
    z = jnp.dot(x.astype(BF16), win_ref[...], preferred_element_type=F32)
    z = 0.5 * z * (1.0 + lax.erf(z * (2.0 ** -0.5)))
    u = z[:, :SGU_WIDTH]
    v = _layer_norm(z[:, SGU_WIDTH:], ng_ref[...], nb_ref[...]).astype(BF16)
    gated = []
    for c in range(tm // SGU_CHUNK):
        rows = slice(c * SGU_CHUNK, (c + 1) * SGU_CHUNK)
        parts = []
        for grp in range(SGU_GROUPS):
            cols = slice(grp * SGU_GDIM, (grp + 1) * SGU_GDIM)
            s = jnp.dot(ws_ref[grp], v[rows, cols], preferred_element_type=F32) + bs_ref[grp]
            parts.append((u[rows, cols] * s).astype(BF16))
        gated.append(jnp.concatenate(parts, axis=1))
    gated = jnp.concatenate(gated, axis=0)
    y = jnp.dot(gated, wout_ref[...], preferred_element_type=F32)
    h = _layer_norm(DN_ALPHA * x + y, g_ref[...], b_ref[...])
    o_ref[...] = _swiglu_ln(h, wg_ref, wu_ref, wd_ref, fg_ref, fb_ref)


def _odd_layer(x, win, ng, nb, ws, bs, wout, g, b, wg, wu, wd, fg, fb, i, layer):
    S = x.shape[0]
    tm = SGU_ROWS
    row = lambda r: (r, 0)
    return pl.pallas_call(
        _odd_kernel,
        grid=(S // tm,),
        in_specs=[pl.BlockSpec((tm, D_MODEL), row),
                  _layer_of(win.shape, i), _layer_of(ng.shape, i), _layer_of(nb.shape, i),
                  _layer_of(ws.shape, i), _layer_of(bs.shape, i), _layer_of(wout.shape, i),
                  _layer_of(g.shape, layer), _layer_of(b.shape, layer),
                  _layer_of(wg.shape, layer), _layer_of(wu.shape, layer), _layer_of(wd.shape, layer),
                  _layer_of(fg.shape, layer), _layer_of(fb.shape, layer)],
        out_specs=pl.BlockSpec((tm, D_MODEL), row),
        out_shape=jax.ShapeDtypeStruct((S, D_MODEL), F32),
        compiler_params=_params(1),
        name="odd_sgu_ln_swiglu_ln",
    )(x, win, ng, nb, ws, bs, wout, g, b, wg, wu, wd, fg, fb)


def _swiglu_ln(x, wg_ref, wu_ref, wd_ref, g_ref, b_ref):
    xb = x.astype(BF16)
    y = None
    start = 0
    for width in FF_CHUNKS:
        cols = slice(start, start + width)
        start += width
        gate = jnp.dot(xb, wg_ref[:, cols], preferred_element_type=F32)
        up = jnp.dot(xb, wu_ref[:, cols], preferred_element_type=F32)
        hid = (jax.nn.silu(gate) * up).astype(BF16)
        part = jnp.dot(hid, wd_ref[cols, :], preferred_element_type=F32)
        y = part if y is None else y + part
    return _layer_norm(DN_ALPHA * x + y, g_ref[...], b_ref[...])


def _ffn_kernel(x_ref, wg_ref, wu_ref, wd_ref, g_ref, b_ref, o_ref):
    o_ref[...] = _swiglu_ln(x_ref[...], wg_ref, wu_ref, wd_ref, g_ref, b_ref)


def _ffn(x, wg, wu, wd, g, b, layer):
    S = x.shape[0]
    tm = ROW_TILE
    row = lambda r: (r, 0)
    return pl.pallas_call(
        _ffn_kernel,
        grid=(S // tm,),
        in_specs=[pl.BlockSpec((tm, D_MODEL), row),
                  _layer_of(wg.shape, layer), _layer_of(wu.shape, layer), _layer_of(wd.shape, layer),
                  _layer_of(g.shape, layer), _layer_of(b.shape, layer)],
        out_specs=pl.BlockSpec((tm, D_MODEL), row),
        out_shape=jax.ShapeDtypeStruct((S, D_MODEL), F32),
        compiler_params=_params(1),
        name="swiglu_ln",
    )(x, wg, wu, wd, g, b)


def _rotary_tables(seq):
    inv = 1.0 / (ROPE_THETA ** (jnp.arange(0, ROPE, 2, dtype=F32) / ROPE))
    ang = jnp.arange(seq, dtype=F32)[:, None] * inv[None, :]
    cos, sin = lax.optimization_barrier((jnp.cos(ang), jnp.sin(ang)))
    eye = np.eye(ROPE // 2, dtype=np.float32)
    rep_c = jnp.asarray(np.concatenate([eye, eye, eye, eye], axis=1))
    rep_s = jnp.asarray(np.concatenate([-eye, eye, -eye, eye], axis=1))
    cc = jnp.dot(cos, rep_c, precision=lax.Precision.HIGHEST)
    ss = jnp.dot(sin, rep_s, precision=lax.Precision.HIGHEST)
    return cos.T, sin.T, cc, ss


def _twiddle_tables():
    k2 = jnp.arange(DFT_N, dtype=jnp.int32)[:, None]
    n1 = jnp.arange(DFT_N, dtype=jnp.int32)[None, :]
    ang = (k2 * n1).astype(F32) * F32(2.0 * np.pi / SEQ)
    ang = ang.reshape(DFT_N, DFT_N, 1)
    cos, sin = lax.optimization_barrier((jnp.cos(ang), jnp.sin(ang)))
    return (jnp.broadcast_to(cos, (DFT_N, DFT_N, FDIM)),
            jnp.broadcast_to(sin, (DFT_N, DFT_N, FDIM)))


def kernel(x, even_w_in, even_q_norm, even_w_uq, even_kv_norm, even_w_uk, even_w_uv, even_w_out,
           odd_w_in, odd_sgu_norm_g, odd_sgu_norm_b, odd_w_spatial, odd_b_spatial, odd_w_out,
           mix_ln_g, mix_ln_b, ffn_w_gate, ffn_w_up, ffn_w_down, ffn_ln_g, ffn_ln_b):
    B, S, D = x.shape
    assert (B, S, D) == (1, SEQ, D_MODEL)
    cosT, sinT, cc, ss = _rotary_tables(S)
    twc, tws = _twiddle_tables()
    half = ROPE // 2
    c0 = Q_RANK + KV_RANK
    c1 = c0 + ROPE
    w = even_w_in
    even_win = jnp.concatenate(
        [w[..., :c0], w[..., c1:], w[..., c0:c1], w[..., c0 + half:c1], w[..., c0:c0 + half]],
        axis=-1).astype(BF16)
    even_qn = even_q_norm[:, None, :]
    even_kvn = even_kv_norm[:, None, :]
    even_wuqT = jnp.swapaxes(even_w_uq, 1, 2).astype(BF16)
    even_wuk = even_w_uk.astype(BF16)
    even_wuvT = jnp.swapaxes(even_w_uv, 1, 2).astype(BF16)
    even_wo = even_w_out.astype(BF16)
    odd_win = odd_w_in.astype(BF16)
    odd_ng = odd_sgu_norm_g[:, None, :]
    odd_nb = odd_sgu_norm_b[:, None, :]
    odd_ws = odd_w_spatial.astype(BF16)
    odd_bs = odd_b_spatial[..., None]
    odd_wout = odd_w_out.astype(BF16)
    mix_g, mix_b = mix_ln_g[:, None, :], mix_ln_b[:, None, :]
    ffn_g, ffn_b = ffn_ln_g[:, None, :], ffn_ln_b[:, None, :]
    ffn_wg, ffn_wu, ffn_wd = ffn_w_gate.astype(BF16), ffn_w_up.astype(BF16), ffn_w_down.astype(BF16)

    h = x.reshape(S, D)
    for layer in range(DEPTH):
        i = layer // 2
        if layer % 2 == 0:
            qT, k, vT, f, qmax_sq, kmax_sq = _even_in(
                h, even_win, even_qn, even_wuqT, even_kvn, even_wuk, even_wuvT, cosT, sinT, cc, ss, i)
            attn = _attention(_score_bound_flags(qmax_sq, kmax_sq), qT, k, vT)
            four = _fourier_mix(f, twc, tws)
            h = _even_out_ffn(attn, four, h, even_wo, mix_g, mix_b,
                              ffn_wg, ffn_wu, ffn_wd, ffn_g, ffn_b, i, layer)
        else:
            h = _odd_layer(h, odd_win, odd_ng, odd_nb, odd_ws, odd_bs, odd_wout, mix_g, mix_b,
                           ffn_wg, ffn_wu, ffn_wd, ffn_g, ffn_b, i, layer)
    return h.reshape(B, S, D)
```

```python
import functools
import math

import numpy as np
import jax
import jax.numpy as jnp
from jax import lax
from jax.experimental import pallas as pl
from jax.experimental.pallas import tpu as pltpu

F32 = jnp.float32
BF16 = jnp.bfloat16

D_MODEL = 1024
SEQ = 16384
DEPTH = 4
HEADS = 8
NOPE = 128
ROPE = 64
QK_DIM = NOPE + ROPE
V_DIM = 128
Q_RANK = 384
KV_RANK = 256
ROPE_THETA = 10000.0
FGROUPS = 4
FDIM = 128
FWIDTH = FGROUPS * FDIM
SGU_CHUNK = 128
SGU_GROUPS = 8
SGU_WIDTH = 2 * D_MODEL
SGU_GDIM = SGU_WIDTH // SGU_GROUPS
D_FF = 2816
DN_ALPHA = (2 * DEPTH) ** 0.25
LN_EPS = 1e-5
RMS_EPS = 1e-6

LANES = 128
VMEM_LIMIT_BYTES = 56 * 1024 * 1024
DFT_N = 128
DFT_ROWS = 16

ROW_TILE = 512
KV_TILE = 512
Q_TILE = 4096
GENERAL_Q = 2048
FF_CHUNKS = (1536, 1280)
SGU_ROWS = 512
ATTN_UNROLL = 5
SCORE_BOUND_LOG2 = 60.0
BOUND_MARGIN = 1.05

_NT = (((1,), (1,)), ((), ()))


def _params(n_axes):
    return pltpu.CompilerParams(
        dimension_semantics=("arbitrary",) * n_axes, vmem_limit_bytes=VMEM_LIMIT_BYTES)


def _resident(shape):
    nd = len(shape)
    return pl.BlockSpec(shape, lambda *_: (0,) * nd, pipeline_mode=pl.Buffered(1))


def _layer_of(stacked_shape, layer):
    nd = len(stacked_shape)
    return pl.BlockSpec((None,) + tuple(stacked_shape[1:]), lambda *_: (layer,) + (0,) * (nd - 1),
                        pipeline_mode=pl.Buffered(1))


def _layer_norm(z, g, b):
    mu = jnp.mean(z, axis=-1, keepdims=True)
    zc = z - mu
    var = jnp.mean(zc * zc, axis=-1, keepdims=True)
    return zc * lax.rsqrt(var + LN_EPS) * g + b


def _rms_norm(z, g):
    ms = jnp.mean(z * z, axis=-1, keepdims=True)
    return z * lax.rsqrt(ms + RMS_EPS) * g


def _even_in_kernel(x_ref, win_ref, qn_ref, wuqT_ref, kvn_ref, wuk_ref, wuvT_ref,
                    cosT_ref, sinT_ref, cc_ref, ss_ref,
                    qT_ref, k_ref, vT_ref, f_ref, qmax_ref, kmax_ref, *, q_scale):
    tm = x_ref.shape[0]
    n_kv = tm // KV_TILE
    xb = x_ref[...].astype(BF16)
    h = jnp.dot(xb, win_ref[...], preferred_element_type=F32)
    cq = _rms_norm(h[:, :Q_RANK], qn_ref[...]).astype(BF16)
    ckv = _rms_norm(h[:, Q_RANK:Q_RANK + KV_RANK], kvn_ref[...]).astype(BF16)
    f_ref[...] = h[:, Q_RANK + KV_RANK:Q_RANK + KV_RANK + FWIDTH]
    kr = h[:, Q_RANK + KV_RANK + FWIDTH:]
    kr_rot = kr * cc_ref[...] + pltpu.roll(kr, ROPE, axis=1) * ss_ref[...]
    kr_rot = kr_rot[:, :ROPE]
    kr_sq = jnp.sum(kr_rot * kr_rot, axis=1, keepdims=True)
    kr_rot = kr_rot.astype(BF16)

    qT = lax.dot_general(wuqT_ref[...], cq, _NT, preferred_element_type=F32)
    cosT = cosT_ref[...]
    sinT = sinT_ref[...]
    half = ROPE // 2
    for hd in range(HEADS):
        base = hd * QK_DIM
        qT_ref[hd, 0:NOPE, :] = (qT[base:base + NOPE] * q_scale).astype(BF16)
        t1 = qT[base + NOPE:base + NOPE + half]
        t2 = qT[base + NOPE + half:base + QK_DIM]
        qT_ref[hd, NOPE:NOPE + half, :] = ((t1 * cosT - t2 * sinT) * q_scale).astype(BF16)
        qT_ref[hd, NOPE + half:QK_DIM, :] = ((t1 * sinT + t2 * cosT) * q_scale).astype(BF16)
        qh = qT[base:base + QK_DIM] * q_scale
        q_sq = jnp.max(jnp.sum(qh * qh, axis=0, keepdims=True), axis=1, keepdims=True)
        qmax_ref[0, hd:hd + 1, :] = jnp.broadcast_to(q_sq, (1, LANES))

    kn = jnp.dot(ckv, wuk_ref[...], preferred_element_type=F32)
    vT = lax.dot_general(wuvT_ref[...], ckv, _NT, preferred_element_type=F32)
    for hd in range(HEADS):
        kh = kn[:, hd * NOPE:(hd + 1) * NOPE]
        k_sq = jnp.max(jnp.sum(kh * kh, axis=1, keepdims=True) + kr_sq, axis=0, keepdims=True)
        kmax_ref[0, hd:hd + 1, :] = jnp.broadcast_to(k_sq, (1, LANES))
        for c in range(n_kv):
            rows = slice(c * KV_TILE, (c + 1) * KV_TILE)
            k_ref[hd, c, :, 0:NOPE] = kn[rows, hd * NOPE:(hd + 1) * NOPE].astype(BF16)
            k_ref[hd, c, :, NOPE:QK_DIM] = kr_rot[rows]
            vT_ref[hd, c] = vT[hd * V_DIM:(hd + 1) * V_DIM, rows].astype(BF16)


def _even_in(x, win, qn, wuqT, kvn, wuk, wuvT, cosT, sinT, cc, ss, i):
    S = x.shape[0]
    tm = ROW_TILE
    n_kv = tm // KV_TILE
    q_scale = (QK_DIM ** -0.5) * math.log2(math.e)
    row = lambda i: (i, 0)
    return pl.pallas_call(
        functools.partial(_even_in_kernel, q_scale=q_scale),
        grid=(S // tm,),
        in_specs=[
            pl.BlockSpec((tm, D_MODEL), row),
            _layer_of(win.shape, i), _layer_of(qn.shape, i), _layer_of(wuqT.shape, i),
            _layer_of(kvn.shape, i), _layer_of(wuk.shape, i), _layer_of(wuvT.shape, i),
            pl.BlockSpec((ROPE // 2, tm), lambda i: (0, i)),
            pl.BlockSpec((ROPE // 2, tm), lambda i: (0, i)),
            pl.BlockSpec((tm, 2 * ROPE), row),
            pl.BlockSpec((tm, 2 * ROPE), row),
        ],
        out_specs=[
            pl.BlockSpec((HEADS, QK_DIM, tm), lambda i: (0, 0, i)),
            pl.BlockSpec((HEADS, n_kv, KV_TILE, QK_DIM), lambda i: (0, i, 0, 0)),
            pl.BlockSpec((HEADS, n_kv, V_DIM, KV_TILE), lambda i: (0, i, 0, 0)),
            pl.BlockSpec((tm, FWIDTH), row),
            pl.BlockSpec((1, HEADS, LANES), lambda i: (i, 0, 0)),
            pl.BlockSpec((1, HEADS, LANES), lambda i: (i, 0, 0)),
        ],
        out_shape=[
            jax.ShapeDtypeStruct((HEADS, QK_DIM, S), BF16),
            jax.ShapeDtypeStruct((HEADS, S // KV_TILE, KV_TILE, QK_DIM), BF16),
            jax.ShapeDtypeStruct((HEADS, S // KV_TILE, V_DIM, KV_TILE), BF16),
            jax.ShapeDtypeStruct((S, FWIDTH), F32),
            jax.ShapeDtypeStruct((S // tm, HEADS, LANES), F32),
            jax.ShapeDtypeStruct((S // tm, HEADS, LANES), F32),
        ],
        compiler_params=_params(1),
        name="even_in",
    )(x, win, qn, wuqT, kvn, wuk, wuvT, cosT, sinT, cc, ss)


def _attn_bounded(qT_ref, k_ref, vT_ref, p0_ref, p1_ref, acc_ref, l_ref):
    n_kv, tk, _ = k_ref.shape
    tq = qT_ref.shape[1]
    qT = qT_ref[...]

    def probs(j, p_ref):
        s = jnp.dot(k_ref[j], qT, preferred_element_type=F32)
        p = jnp.exp2(s)
        p_ref[...] = p.astype(BF16)
        return jnp.sum(p.reshape(tk // 8, 8, tq), axis=0)

    def values(j, p_ref):
        acc_ref[...] += jnp.dot(vT_ref[j], p_ref[...], preferred_element_type=F32)

    acc_ref[...] = jnp.zeros_like(acc_ref)
    l8 = probs(0, p0_ref)

    def pair(i, l8):
        j = 2 * i
        l8 = l8 + probs(j + 1, p1_ref)
        values(j, p0_ref)
        l8 = l8 + probs(j + 2, p0_ref)
        values(j + 1, p1_ref)
        return l8

    l8 = lax.fori_loop(0, n_kv // 2 - 1, pair, l8, unroll=ATTN_UNROLL)
    l8 = l8 + probs(n_kv - 1, p1_ref)
    values(n_kv - 2, p0_ref)
    values(n_kv - 1, p1_ref)
    l_ref[...] = jnp.sum(l8, axis=0, keepdims=True)


def _attn_general(qT_ref, k_ref, vT_ref, s0_ref, s1_ref, p0_ref, p1_ref, acc_ref, l_ref):
    n_kv = k_ref.shape[0]
    tq = GENERAL_Q
    for seg in range(qT_ref.shape[1] // tq):
        cols = slice(seg * tq, (seg + 1) * tq)
        qT = qT_ref[:, cols]

        def scores(j, s_ref):
            s = jnp.dot(k_ref[j], qT, preferred_element_type=F32)
            s_ref[...] = s
            return jnp.max(s, axis=0, keepdims=True)

        def softmax(s_ref, p_ref, mx, m, l):
            m_new = jnp.maximum(m, mx)
            alpha = jnp.exp2(m - m_new)
            p = jnp.exp2(s_ref[...] - m_new)
            p_ref[:, :tq] = p.astype(BF16)
            return m_new, alpha * l + jnp.sum(p, axis=0, keepdims=True), alpha

        def values(j, p_ref, alpha):
            pv = jnp.dot(vT_ref[j], p_ref[:, :tq], preferred_element_type=F32)
            acc_ref[:, cols] = alpha * acc_ref[:, cols] + pv

        p1_ref[:, :tq] = jnp.zeros((p1_ref.shape[0], tq), BF16)
        acc_ref[:, cols] = jnp.zeros((acc_ref.shape[0], tq), F32)
        mx0 = scores(0, s0_ref)

        def pair(i, carry):
            mx0, m, l, alpha_prev = carry
            j = 2 * i
            mx1 = scores(j + 1, s1_ref)
            m, l, alpha0 = softmax(s0_ref, p0_ref, mx0, m, l)
            values(jnp.maximum(j - 1, 0), p1_ref, alpha_prev)
            mx0 = scores(jnp.minimum(j + 2, n_kv - 1), s0_ref)
            m, l, alpha1 = softmax(s1_ref, p1_ref, mx1, m, l)
            values(j, p0_ref, alpha0)
            return mx0, m, l, alpha1

        m0 = jnp.full((1, tq), -jnp.inf, F32)
        l0 = jnp.zeros((1, tq), F32)
        _, _, l, alpha = lax.fori_loop(0, n_kv // 2, pair, (mx0, m0, l0, jnp.ones((1, tq), F32)))
        values(n_kv - 1, p1_ref, alpha)
        l_ref[:, cols] = l


def _attn_kernel(bounded_ref, qT_ref, k_ref, vT_ref, o_ref,
                 s0_ref, s1_ref, p0_ref, p1_ref, acc_ref, l_ref):
    bounded = bounded_ref[pl.program_id(0), pl.program_id(1)] != 0

    @pl.when(bounded)
    def _():
        _attn_bounded(qT_ref, k_ref, vT_ref, p0_ref, p1_ref, acc_ref, l_ref)

    @pl.when(jnp.logical_not(bounded))
    def _():
        _attn_general(qT_ref, k_ref, vT_ref, s0_ref, s1_ref, p0_ref, p1_ref, acc_ref, l_ref)

    o_ref[...] = (acc_ref[...] * (1.0 / l_ref[...])).T.astype(o_ref.dtype)


def _attention(bounded, qT, k, vT):
    H, n_kv, tk, _ = k.shape
    S = qT.shape[2]
    tq = Q_TILE
    assert n_kv % 2 == 0
    return pl.pallas_call(
        _attn_kernel,
        grid_spec=pltpu.PrefetchScalarGridSpec(
            num_scalar_prefetch=1,
            grid=(H, S // tq),
            in_specs=[
                pl.BlockSpec((None, QK_DIM, tq), lambda h, i, b: (h, 0, i)),
                pl.BlockSpec((None, n_kv, tk, QK_DIM), lambda h, i, b: (h, 0, 0, 0)),
                pl.BlockSpec((None, n_kv, V_DIM, tk), lambda h, i, b: (h, 0, 0, 0)),
            ],
            out_specs=pl.BlockSpec((tq, V_DIM), lambda h, i, b: (i, h)),
            scratch_shapes=[pltpu.VMEM((tk, GENERAL_Q), F32), pltpu.VMEM((tk, GENERAL_Q), F32),
                            pltpu.VMEM((tk, tq), BF16), pltpu.VMEM((tk, tq), BF16),
                            pltpu.VMEM((V_DIM, tq), F32), pltpu.VMEM((1, tq), F32)],
        ),
        out_shape=jax.ShapeDtypeStruct((S, H * V_DIM), BF16),
        compiler_params=_params(2),
        name="mla_attention",
    )(bounded, qT, k, vT)


def _score_bound_flags(qmax_sq, kmax_sq):
    n_blk = qmax_sq.shape[0]
    per = Q_TILE // ROW_TILE
    q_sq = jnp.max(qmax_sq[:, :, 0].reshape(n_blk // per, per, HEADS), axis=1)
    k_sq = jnp.max(kmax_sq[:, :, 0], axis=0)
    ok = q_sq * k_sq[None, :] * (BOUND_MARGIN ** 2) <= SCORE_BOUND_LOG2 ** 2
    return ok.astype(jnp.int32).T


def _dft_consts():
    n = np.arange(DFT_N)
    ang = 2.0 * np.pi * np.outer(n, n) / DFT_N
    c = np.cos(ang) / np.sqrt(DFT_N)
    s = np.sin(ang) / np.sqrt(DFT_N)
    w_a = np.concatenate([c, -s], axis=0)
    w_b = np.block([[c, s], [-s, c]])
    w_c = np.concatenate([c, s], axis=0)
    return tuple(jnp.asarray(w, F32).astype(BF16) for w in (w_a, w_b, w_c))


def _dft_a_kernel(x_ref, w_ref, tr_ref, ti_ref, xs_ref, ts_ref):
    w = w_ref[...]
    for r in range(DFT_ROWS):
        xs_ref[r] = x_ref[:, r, :]
    for r in range(DFT_ROWS):
        ts_ref[r] = jnp.dot(w, xs_ref[r].astype(BF16), preferred_element_type=F32)
    for r in range(DFT_ROWS):
        tr_ref[:, r, :] = ts_ref[r, :DFT_N, :]
        ti_ref[:, r, :] = ts_ref[r, DFT_N:, :]


def _dft_b_kernel(tr_ref, ti_ref, twc_ref, tws_ref, wb_ref, wc_ref, y_ref, ys_ref):
    re, im = [], []
    for j in range(DFT_ROWS):
        tr, ti = tr_ref[j], ti_ref[j]
        cw = jnp.concatenate([twc_ref[j]] * FGROUPS, axis=1)
        sw = jnp.concatenate([tws_ref[j]] * FGROUPS, axis=1)
        re.append((tr * cw + ti * sw).astype(BF16))
        im.append((ti * cw - tr * sw).astype(BF16))
    st = jnp.concatenate([jnp.concatenate(re, axis=1), jnp.concatenate(im, axis=1)], axis=0)
    g = jnp.dot(wb_ref[...], st, preferred_element_type=F32).astype(BF16)
    n_blk = DFT_ROWS * FGROUPS
    gg = jnp.concatenate(
        [jnp.concatenate([g[:DFT_N, b * FDIM:(b + 1) * FDIM], g[DFT_N:, b * FDIM:(b + 1) * FDIM]], axis=1)
         for b in range(n_blk)], axis=0)
    y = jnp.dot(gg, wc_ref[...], preferred_element_type=F32)
    for j in range(DFT_ROWS):
        for grp in range(FGROUPS):
            b = j * FGROUPS + grp
            ys_ref[j, :, grp * FDIM:(grp + 1) * FDIM] = y[b * DFT_N:(b + 1) * DFT_N]
    for j in range(DFT_ROWS):
        y_ref[:, j, :] = ys_ref[j]


def _fourier_mix(f, twc, tws):
    S = f.shape[0]
    w_a, w_b, w_c = _dft_consts()
    cube = (DFT_N, DFT_N, FWIDTH)
    steps = (DFT_N // DFT_ROWS,)
    inner = pl.BlockSpec((DFT_N, DFT_ROWS, FWIDTH), lambda i: (0, i, 0))
    outer = pl.BlockSpec((DFT_ROWS, DFT_N, FWIDTH), lambda i: (i, 0, 0))
    tr, ti = pl.pallas_call(
        _dft_a_kernel,
        grid=steps,
        in_specs=[inner, _resident(w_a.shape)],
        out_specs=[inner, inner],
        out_shape=[jax.ShapeDtypeStruct(cube, F32)] * 2,
        scratch_shapes=[pltpu.VMEM((DFT_ROWS, DFT_N, FWIDTH), F32),
                        pltpu.VMEM((DFT_ROWS, 2 * DFT_N, FWIDTH), F32)],
        compiler_params=_params(1),
        name="dft_positions_outer",
    )(f.reshape(cube), w_a)
    y = pl.pallas_call(
        _dft_b_kernel,
        grid=steps,
        in_specs=[outer, outer,
                  pl.BlockSpec((DFT_ROWS, DFT_N, FDIM), lambda i: (i, 0, 0)),
                  pl.BlockSpec((DFT_ROWS, DFT_N, FDIM), lambda i: (i, 0, 0)),
                  _resident(w_b.shape), _resident(w_c.shape)],
        out_specs=inner,
        out_shape=jax.ShapeDtypeStruct(cube, F32),
        scratch_shapes=[pltpu.VMEM((DFT_ROWS, DFT_N, FWIDTH), F32)],
        compiler_params=_params(1),
        name="dft_positions_inner_channels",
    )(tr, ti, twc, tws, w_b, w_c)
    return y.reshape(S, FWIDTH)


def _even_out_ffn_kernel(a_ref, f_ref, x_ref, wa_ref, wf_ref, g_ref, b_ref,
                         wg_ref, wu_ref, wd_ref, fg_ref, fb_ref, o_ref):
    y = jnp.dot(a_ref[...], wa_ref[...], preferred_element_type=F32)
    y = y + jnp.dot(f_ref[...].astype(BF16), wf_ref[...], preferred_element_type=F32)
    h = _layer_norm(DN_ALPHA * x_ref[...] + y, g_ref[...], b_ref[...])
    o_ref[...] = _swiglu_ln(h, wg_ref, wu_ref, wd_ref, fg_ref, fb_ref)


def _even_out_ffn(attn, four, x, wo, g, b, wg, wu, wd, fg, fb, i, layer):
    S = x.shape[0]
    tm = ROW_TILE
    row = lambda r: (r, 0)
    n_attn = HEADS * V_DIM
    wa = pl.BlockSpec((None, n_attn, D_MODEL), lambda *_: (i, 0, 0), pipeline_mode=pl.Buffered(1))
    wf = pl.BlockSpec((None, FWIDTH, D_MODEL), lambda *_: (i, n_attn // FWIDTH, 0),
                      pipeline_mode=pl.Buffered(1))
    return pl.pallas_call(
        _even_out_ffn_kernel,
        grid=(S // tm,),
        in_specs=[pl.BlockSpec((tm, n_attn), row), pl.BlockSpec((tm, FWIDTH), row),
                  pl.BlockSpec((tm, D_MODEL), row),
                  wa, wf, _layer_of(g.shape, layer), _layer_of(b.shape, layer),
                  _layer_of(wg.shape, layer), _layer_of(wu.shape, layer), _layer_of(wd.shape, layer),
                  _layer_of(fg.shape, layer), _layer_of(fb.shape, layer)],
        out_specs=pl.BlockSpec((tm, D_MODEL), row),
        out_shape=jax.ShapeDtypeStruct((S, D_MODEL), F32),
        compiler_params=_params(1),
        name="even_out_ln_swiglu_ln",
    )(attn, four, x, wo, wo, g, b, wg, wu, wd, fg, fb)


def _odd_kernel(x_ref, win_ref, ng_ref, nb_ref, ws_ref, bs_ref, wout_ref, g_ref, b_ref,
                wg_ref, wu_ref, wd_ref, fg_ref, fb_ref, o_ref):
    tm = x_ref.shape[0]
    x = x_ref[...]
    z = jnp.dot(x.astype(BF16), win_ref[...], preferred_element_type=F32)
    z = 0.5 * z * (1.0 + lax.erf(z * (2.0 ** -0.5)))
    u = z[:, :SGU_WIDTH]
    v = _layer_norm(z[:, SGU_WIDTH:], ng_ref[...], nb_ref[...]).astype(BF16)
    gated = []
    for c in range(tm // SGU_CHUNK):
        rows = slice(c * SGU_CHUNK, (c + 1) * SGU_CHUNK)
        parts = []
        for grp in range(SGU_GROUPS):
            cols = slice(grp * SGU_GDIM, (grp + 1) * SGU_GDIM)
            s = jnp.dot(ws_ref[grp], v[rows, cols], preferred_element_type=F32) + bs_ref[grp]
            parts.append((u[rows, cols] * s).astype(BF16))
        gated.append(jnp.concatenate(parts, axis=1))
    gated = jnp.concatenate(gated, axis=0)
    y = jnp.dot(gated, wout_ref[...], preferred_element_type=F32)
    h = _layer_norm(DN_ALPHA * x + y, g_ref[...], b_ref[...])
    o_ref[...] = _swiglu_ln(h, wg_ref, wu_ref, wd_ref, fg_ref, fb_ref)


def _odd_layer(x, win, ng, nb, ws, bs, wout, g, b, wg, wu, wd, fg, fb, i, layer):
    S = x.shape[0]
    tm = SGU_ROWS
    row = lambda r: (r, 0)
    return pl.pallas_call(
        _odd_kernel,
        grid=(S // tm,),
        in_specs=[pl.BlockSpec((tm, D_MODEL), row),
                  _layer_of(win.shape, i), _layer_of(ng.shape, i), _layer_of(nb.shape, i),
                  _layer_of(ws.shape, i), _layer_of(bs.shape, i), _layer_of(wout.shape, i),
                  _layer_of(g.shape, layer), _layer_of(b.shape, layer),
                  _layer_of(wg.shape, layer), _layer_of(wu.shape, layer), _layer_of(wd.shape, layer),
                  _layer_of(fg.shape, layer), _layer_of(fb.shape, layer)],
        out_specs=pl.BlockSpec((tm, D_MODEL), row),
        out_shape=jax.ShapeDtypeStruct((S, D_MODEL), F32),
        compiler_params=_params(1),
        name="odd_sgu_ln_swiglu_ln",
    )(x, win, ng, nb, ws, bs, wout, g, b, wg, wu, wd, fg, fb)


def _swiglu_ln(x, wg_ref, wu_ref, wd_ref, g_ref, b_ref):
    xb = x.astype(BF16)
    y = None
    start = 0
    for width in FF_CHUNKS:
        cols = slice(start, start + width)
        start += width
        gate = jnp.dot(xb, wg_ref[:, cols], preferred_element_type=F32)
        up = jnp.dot(xb, wu_ref[:, cols], preferred_element_type=F32)
        hid = (jax.nn.silu(gate) * up).astype(BF16)
        part = jnp.dot(hid, wd_ref[cols, :], preferred_element_type=F32)
        y = part if y is None else y + part
    return _layer_norm(DN_ALPHA * x + y, g_ref[...], b_ref[...])


def _rotary_tables(seq):
    inv = 1.0 / (ROPE_THETA ** (jnp.arange(0, ROPE, 2, dtype=F32) / ROPE))
    ang = jnp.arange(seq, dtype=F32)[:, None] * inv[None, :]
    cos, sin = lax.optimization_barrier((jnp.cos(ang), jnp.sin(ang)))
    eye = np.eye(ROPE // 2, dtype=np.float32)
    rep_c = jnp.asarray(np.concatenate([eye, eye, eye, eye], axis=1))
    rep_s = jnp.asarray(np.concatenate([-eye, eye, -eye, eye], axis=1))
    cc = jnp.dot(cos, rep_c, precision=lax.Precision.HIGHEST)
    ss = jnp.dot(sin, rep_s, precision=lax.Precision.HIGHEST)
    return cos.T, sin.T, cc, ss


def _twiddle_tables():
    k2 = jnp.arange(DFT_N, dtype=jnp.int32)[:, None]
    n1 = jnp.arange(DFT_N, dtype=jnp.int32)[None, :]
    ang = (k2 * n1).astype(F32) * F32(2.0 * np.pi / SEQ)
    ang = ang.reshape(DFT_N, DFT_N, 1)
    cos, sin = lax.optimization_barrier((jnp.cos(ang), jnp.sin(ang)))
    return (jnp.broadcast_to(cos, (DFT_N, DFT_N, FDIM)),
            jnp.broadcast_to(sin, (DFT_N, DFT_N, FDIM)))


def kernel(x, even_w_in, even_q_norm, even_w_uq, even_kv_norm, even_w_uk, even_w_uv, even_w_out,
           odd_w_in, odd_sgu_norm_g, odd_sgu_norm_b, odd_w_spatial, odd_b_spatial, odd_w_out,
           mix_ln_g, mix_ln_b, ffn_w_gate, ffn_w_up, ffn_w_down, ffn_ln_g, ffn_ln_b):
    B, S, D = x.shape
    assert (B, S, D) == (1, SEQ, D_MODEL)
    cosT, sinT, cc, ss = _rotary_tables(S)
    twc, tws = _twiddle_tables()
    half = ROPE // 2
    c0 = Q_RANK + KV_RANK
    c1 = c0 + ROPE
    w = even_w_in
    even_win = jnp.concatenate(
        [w[..., :c0], w[..., c1:], w[..., c0:c1], w[..., c0 + half:c1], w[..., c0:c0 + half]],
        axis=-1).astype(BF16)
    even_qn = even_q_norm[:, None, :]
    even_kvn = even_kv_norm[:, None, :]
    even_wuqT = jnp.swapaxes(even_w_uq, 1, 2).astype(BF16)
    even_wuk = even_w_uk.astype(BF16)
    even_wuvT = jnp.swapaxes(even_w_uv, 1, 2).astype(BF16)
    even_wo = even_w_out.astype(BF16)
    odd_win = odd_w_in.astype(BF16)
    odd_ng = odd_sgu_norm_g[:, None, :]
    odd_nb = odd_sgu_norm_b[:, None, :]
    odd_ws = odd_w_spatial.astype(BF16)
    odd_bs = odd_b_spatial[..., None]
    odd_wout = odd_w_out.astype(BF16)
    mix_g, mix_b = mix_ln_g[:, None, :], mix_ln_b[:, None, :]
    ffn_g, ffn_b = ffn_ln_g[:, None, :], ffn_ln_b[:, None, :]
    ffn_wg, ffn_wu, ffn_wd = ffn_w_gate.astype(BF16), ffn_w_up.astype(BF16), ffn_w_down.astype(BF16)

    h = x.reshape(S, D)
    for layer in range(DEPTH):
        i = layer // 2
        if layer % 2 == 0:
            qT, k, vT, f, qmax_sq, kmax_sq = _even_in(
                h, even_win, even_qn, even_wuqT, even_kvn, even_wuk, even_wuvT, cosT, sinT, cc, ss, i)
            attn = _attention(_score_bound_flags(qmax_sq, kmax_sq), qT, k, vT)
            four = _fourier_mix(f, twc, tws)
            h = _even_out_ffn(attn, four, h, even_wo, mix_g, mix_b,
                              ffn_wg, ffn_wu, ffn_wd, ffn_g, ffn_b, i, layer)
        else:
            h = _odd_layer(h, odd_win, odd_ng, odd_nb, odd_ws, odd_bs, odd_wout, mix_g, mix_b,
                           ffn_wg, ffn_wu, ffn_wd, ffn_g, ffn_b, i, layer)
    return h.reshape(B, S, D)
```

```python
import functools
import math

import numpy as np
import jax
import jax.numpy as jnp
from jax import lax
from jax.experimental import pallas as pl
from jax.experimental.pallas import tpu as pltpu

F32 = jnp.float32
BF16 = jnp.bfloat16

D_MODEL = 1024
SEQ = 16384
DEPTH = 4
HEADS = 8
NOPE = 128
ROPE = 64
QK_DIM = NOPE + ROPE
V_DIM = 128
Q_RANK = 384
KV_RANK = 256
ROPE_THETA = 10000.0
FGROUPS = 4
FDIM = 128
FWIDTH = FGROUPS * FDIM
SGU_CHUNK = 128
SGU_GROUPS = 8
SGU_WIDTH = 2 * D_MODEL
SGU_GDIM = SGU_WIDTH // SGU_GROUPS
D_FF = 2816
DN_ALPHA = (2 * DEPTH) ** 0.25
LN_EPS = 1e-5
RMS_EPS = 1e-6

LANES = 128
VMEM_LIMIT_BYTES = 56 * 1024 * 1024
DFT_N = 128
DFT_ROWS = 16

ROW_TILE = 512
KV_TILE = 512
Q_TILE = 4096
GENERAL_Q = 2048
FF_CHUNKS = (1536, 1280)
SGU_ROWS = 512
ATTN_UNROLL = 5
SCORE_BOUND_LOG2 = 60.0
VALUE_ABS_BOUND = 2.0 ** 40
BOUND_MARGIN = 1.05

_NT = (((1,), (1,)), ((), ()))


def _params(n_axes):
    return pltpu.CompilerParams(
        dimension_semantics=("arbitrary",) * n_axes, vmem_limit_bytes=VMEM_LIMIT_BYTES)


def _resident(shape):
    nd = len(shape)
    return pl.BlockSpec(shape, lambda *_: (0,) * nd, pipeline_mode=pl.Buffered(1))


def _layer_of(stacked_shape, layer):
    nd = len(stacked_shape)
    return pl.BlockSpec((None,) + tuple(stacked_shape[1:]), lambda *_: (layer,) + (0,) * (nd - 1),
                        pipeline_mode=pl.Buffered(1))


def _layer_norm(z, g, b):
    mu = jnp.mean(z, axis=-1, keepdims=True)
    zc = z - mu
    var = jnp.mean(zc * zc, axis=-1, keepdims=True)
    return zc * lax.rsqrt(var + LN_EPS) * g + b


def _rms_norm(z, g):
    ms = jnp.mean(z * z, axis=-1, keepdims=True)
    return z * lax.rsqrt(ms + RMS_EPS) * g


def _even_in_kernel(x_ref, win_ref, qn_ref, wuqT_ref, kvn_ref, wuk_ref, wuvT_ref,
                    cosT_ref, sinT_ref, cc_ref, ss_ref,
                    qT_ref, k_ref, vT_ref, f_ref, qmax_ref, kmax_ref, vmax_ref, *, q_scale):
    tm = x_ref.shape[0]
    n_kv = tm // KV_TILE
    xb = x_ref[...].astype(BF16)
    h = jnp.dot(xb, win_ref[...], preferred_element_type=F32)
    cq = _rms_norm(h[:, :Q_RANK], qn_ref[...]).astype(BF16)
    ckv = _rms_norm(h[:, Q_RANK:Q_RANK + KV_RANK], kvn_ref[...]).astype(BF16)
    f_ref[...] = h[:, Q_RANK + KV_RANK:Q_RANK + KV_RANK + FWIDTH]
    kr = h[:, Q_RANK + KV_RANK + FWIDTH:]
    kr_rot = kr * cc_ref[...] + pltpu.roll(kr, ROPE, axis=1) * ss_ref[...]
    kr_rot = kr_rot[:, :ROPE]
    kr_sq = jnp.sum(kr_rot * kr_rot, axis=1, keepdims=True)
    kr_rot = kr_rot.astype(BF16)

    qT = lax.dot_general(wuqT_ref[...], cq, _NT, preferred_element_type=F32)
    cosT = cosT_ref[...]
    sinT = sinT_ref[...]
    half = ROPE // 2
    for hd in range(HEADS):
        base = hd * QK_DIM
        qT_ref[hd, 0:NOPE, :] = (qT[base:base + NOPE] * q_scale).astype(BF16)
        t1 = qT[base + NOPE:base + NOPE + half]
        t2 = qT[base + NOPE + half:base + QK_DIM]
        qT_ref[hd, NOPE:NOPE + half, :] = ((t1 * cosT - t2 * sinT) * q_scale).astype(BF16)
        qT_ref[hd, NOPE + half:QK_DIM, :] = ((t1 * sinT + t2 * cosT) * q_scale).astype(BF16)
        qh = qT[base:base + QK_DIM] * q_scale
        q_sq = jnp.max(jnp.sum(qh * qh, axis=0, keepdims=True), axis=1, keepdims=True)
        qmax_ref[0, hd:hd + 1, :] = jnp.broadcast_to(q_sq, (1, LANES))

    kn = jnp.dot(ckv, wuk_ref[...], preferred_element_type=F32)
    vT = lax.dot_general(wuvT_ref[...], ckv, _NT, preferred_element_type=F32)
    for hd in range(HEADS):
        kh = kn[:, hd * NOPE:(hd + 1) * NOPE]
        k_sq = jnp.max(jnp.sum(kh * kh, axis=1, keepdims=True) + kr_sq, axis=0, keepdims=True)
        kmax_ref[0, hd:hd + 1, :] = jnp.broadcast_to(k_sq, (1, LANES))
        v_abs = jnp.max(jnp.abs(vT[hd * V_DIM:(hd + 1) * V_DIM]), axis=0, keepdims=True)
        vmax_ref[0, hd:hd + 1, :] = jnp.broadcast_to(jnp.max(v_abs, axis=1, keepdims=True), (1, LANES))
        for c in range(n_kv):
            rows = slice(c * KV_TILE, (c + 1) * KV_TILE)
            k_ref[hd, c, :, 0:NOPE] = kn[rows, hd * NOPE:(hd + 1) * NOPE].astype(BF16)
            k_ref[hd, c, :, NOPE:QK_DIM] = kr_rot[rows]
            vT_ref[hd, c] = vT[hd * V_DIM:(hd + 1) * V_DIM, rows].astype(BF16)


def _even_in(x, win, qn, wuqT, kvn, wuk, wuvT, cosT, sinT, cc, ss, i):
    S = x.shape[0]
    tm = ROW_TILE
    n_kv = tm // KV_TILE
    q_scale = (QK_DIM ** -0.5) * math.log2(math.e)
    row = lambda i: (i, 0)
    return pl.pallas_call(
        functools.partial(_even_in_kernel, q_scale=q_scale),
        grid=(S // tm,),
        in_specs=[
            pl.BlockSpec((tm, D_MODEL), row),
            _layer_of(win.shape, i), _layer_of(qn.shape, i), _layer_of(wuqT.shape, i),
            _layer_of(kvn.shape, i), _layer_of(wuk.shape, i), _layer_of(wuvT.shape, i),
            pl.BlockSpec((ROPE // 2, tm), lambda i: (0, i)),
            pl.BlockSpec((ROPE // 2, tm), lambda i: (0, i)),
            pl.BlockSpec((tm, 2 * ROPE), row),
            pl.BlockSpec((tm, 2 * ROPE), row),
        ],
        out_specs=[
            pl.BlockSpec((HEADS, QK_DIM, tm), lambda i: (0, 0, i)),
            pl.BlockSpec((HEADS, n_kv, KV_TILE, QK_DIM), lambda i: (0, i, 0, 0)),
            pl.BlockSpec((HEADS, n_kv, V_DIM, KV_TILE), lambda i: (0, i, 0, 0)),
            pl.BlockSpec((tm, FWIDTH), row),
            pl.BlockSpec((1, HEADS, LANES), lambda i: (i, 0, 0)),
            pl.BlockSpec((1, HEADS, LANES), lambda i: (i, 0, 0)),
            pl.BlockSpec((1, HEADS, LANES), lambda i: (i, 0, 0)),
        ],
        out_shape=[
            jax.ShapeDtypeStruct((HEADS, QK_DIM, S), BF16),
            jax.ShapeDtypeStruct((HEADS, S // KV_TILE, KV_TILE, QK_DIM), BF16),
            jax.ShapeDtypeStruct((HEADS, S // KV_TILE, V_DIM, KV_TILE), BF16),
            jax.ShapeDtypeStruct((S, FWIDTH), F32),
            jax.ShapeDtypeStruct((S // tm, HEADS, LANES), F32),
            jax.ShapeDtypeStruct((S // tm, HEADS, LANES), F32),
            jax.ShapeDtypeStruct((S // tm, HEADS, LANES), F32),
        ],
        compiler_params=_params(1),
        name="even_in",
    )(x, win, qn, wuqT, kvn, wuk, wuvT, cosT, sinT, cc, ss)


def _attn_bounded(qT_ref, k_ref, vT_ref, p0_ref, p1_ref, acc_ref, l_ref):
    n_kv, tk, _ = k_ref.shape
    tq = qT_ref.shape[1]
    qT = qT_ref[...]

    def probs(j, p_ref):
        s = jnp.dot(k_ref[j], qT, preferred_element_type=F32)
        p = jnp.exp2(s)
        p_ref[...] = p.astype(BF16)
        return jnp.sum(p.reshape(tk // 8, 8, tq), axis=0)

    def values(j, p_ref):
        acc_ref[...] += jnp.dot(vT_ref[j], p_ref[...], preferred_element_type=F32)

    acc_ref[...] = jnp.zeros_like(acc_ref)
    l8 = probs(0, p0_ref)

    def pair(i, l8):
        j = 2 * i
        l8 = l8 + probs(j + 1, p1_ref)
        values(j, p0_ref)
        l8 = l8 + probs(j + 2, p0_ref)
        values(j + 1, p1_ref)
        return l8

    l8 = lax.fori_loop(0, n_kv // 2 - 1, pair, l8, unroll=ATTN_UNROLL)
    l8 = l8 + probs(n_kv - 1, p1_ref)
    values(n_kv - 2, p0_ref)
    values(n_kv - 1, p1_ref)
    l_ref[...] = jnp.sum(l8, axis=0, keepdims=True)


def _attn_general(qT_ref, k_ref, vT_ref, s0_ref, s1_ref, p0_ref, p1_ref, acc_ref, l_ref):
    n_kv = k_ref.shape[0]
    tq = GENERAL_Q
    for seg in range(qT_ref.shape[1] // tq):
        cols = slice(seg * tq, (seg + 1) * tq)
        qT = qT_ref[:, cols]

        def scores(j, s_ref):
            s = jnp.dot(k_ref[j], qT, preferred_element_type=F32)
            s_ref[...] = s
            return jnp.max(s, axis=0, keepdims=True)

        def softmax(s_ref, p_ref, mx, m, l):
            m_new = jnp.maximum(m, mx)
            alpha = jnp.exp2(m - m_new)
            p = jnp.exp2(s_ref[...] - m_new)
            p_ref[:, :tq] = p.astype(BF16)
            return m_new, alpha * l + jnp.sum(p, axis=0, keepdims=True), alpha

        def values(j, p_ref, alpha):
            pv = jnp.dot(vT_ref[j], p_ref[:, :tq], preferred_element_type=F32)
            acc_ref[:, cols] = alpha * acc_ref[:, cols] + pv

        p1_ref[:, :tq] = jnp.zeros((p1_ref.shape[0], tq), BF16)
        acc_ref[:, cols] = jnp.zeros((acc_ref.shape[0], tq), F32)
        mx0 = scores(0, s0_ref)

        def pair(i, carry):
            mx0, m, l, alpha_prev = carry
            j = 2 * i
            mx1 = scores(j + 1, s1_ref)
            m, l, alpha0 = softmax(s0_ref, p0_ref, mx0, m, l)
            values(jnp.maximum(j - 1, 0), p1_ref, alpha_prev)
            mx0 = scores(jnp.minimum(j + 2, n_kv - 1), s0_ref)
            m, l, alpha1 = softmax(s1_ref, p1_ref, mx1, m, l)
            values(j, p0_ref, alpha0)
            return mx0, m, l, alpha1

        m0 = jnp.full((1, tq), -jnp.inf, F32)
        l0 = jnp.zeros((1, tq), F32)
        _, _, l, alpha = lax.fori_loop(0, n_kv // 2, pair, (mx0, m0, l0, jnp.ones((1, tq), F32)))
        values(n_kv - 1, p1_ref, alpha)
        l_ref[:, cols] = l


def _attn_kernel(bounded_ref, qT_ref, k_ref, vT_ref, o_ref,
                 s0_ref, s1_ref, p0_ref, p1_ref, acc_ref, l_ref):
    bounded = bounded_ref[pl.program_id(0), pl.program_id(1)] != 0

    @pl.when(bounded)
    def _():
        _attn_bounded(qT_ref, k_ref, vT_ref, p0_ref, p1_ref, acc_ref, l_ref)

    @pl.when(jnp.logical_not(bounded))
    def _():
        _attn_general(qT_ref, k_ref, vT_ref, s0_ref, s1_ref, p0_ref, p1_ref, acc_ref, l_ref)

    o_ref[...] = (acc_ref[...] * (1.0 / l_ref[...])).T.astype(o_ref.dtype)


def _attention(bounded, qT, k, vT):
    H, n_kv, tk, _ = k.shape
    S = qT.shape[2]
    tq = Q_TILE
    assert n_kv % 2 == 0
    return pl.pallas_call(
        _attn_kernel,
        grid_spec=pltpu.PrefetchScalarGridSpec(
            num_scalar_prefetch=1,
            grid=(H, S // tq),
            in_specs=[
                pl.BlockSpec((None, QK_DIM, tq), lambda h, i, b: (h, 0, i)),
                pl.BlockSpec((None, n_kv, tk, QK_DIM), lambda h, i, b: (h, 0, 0, 0)),
                pl.BlockSpec((None, n_kv, V_DIM, tk), lambda h, i, b: (h, 0, 0, 0)),
            ],
            out_specs=pl.BlockSpec((tq, V_DIM), lambda h, i, b: (i, h)),
            scratch_shapes=[pltpu.VMEM((tk, GENERAL_Q), F32), pltpu.VMEM((tk, GENERAL_Q), F32),
                            pltpu.VMEM((tk, tq), BF16), pltpu.VMEM((tk, tq), BF16),
                            pltpu.VMEM((V_DIM, tq), F32), pltpu.VMEM((1, tq), F32)],
        ),
        out_shape=jax.ShapeDtypeStruct((S, H * V_DIM), BF16),
        compiler_params=_params(2),
        name="mla_attention",
    )(bounded, qT, k, vT)


def _score_bound_flags(qmax_sq, kmax_sq, vmax):
    n_blk = qmax_sq.shape[0]
    per = Q_TILE // ROW_TILE
    q_sq = jnp.max(qmax_sq[:, :, 0].reshape(n_blk // per, per, HEADS), axis=1)
    k_sq = jnp.max(kmax_sq[:, :, 0], axis=0)
    ok = q_sq * k_sq[None, :] * (BOUND_MARGIN ** 2) <= SCORE_BOUND_LOG2 ** 2
    ok = ok & (jnp.max(vmax[:, :, 0], axis=0) <= VALUE_ABS_BOUND)[None, :]
    return ok.astype(jnp.int32).T


def _dft_consts():
    n = np.arange(DFT_N)
    ang = 2.0 * np.pi * np.outer(n, n) / DFT_N
    c = np.cos(ang) / np.sqrt(DFT_N)
    s = np.sin(ang) / np.sqrt(DFT_N)
    w_a = np.concatenate([c, -s], axis=0)
    w_b = np.block([[c, s], [-s, c]])
    w_c = np.concatenate([c, s], axis=0)
    return tuple(jnp.asarray(w, F32).astype(BF16) for w in (w_a, w_b, w_c))


def _dft_a_kernel(x_ref, w_ref, tr_ref, ti_ref, xs_ref, ts_ref):
    w = w_ref[...]
    for r in range(DFT_ROWS):
        xs_ref[r] = x_ref[:, r, :]
    for r in range(DFT_ROWS):
        ts_ref[r] = jnp.dot(w, xs_ref[r].astype(BF16), preferred_element_type=F32)
    for r in range(DFT_ROWS):
        tr_ref[:, r, :] = ts_ref[r, :DFT_N, :]
        ti_ref[:, r, :] = ts_ref[r, DFT_N:, :]


def _dft_b_kernel(tr_ref, ti_ref, twc_ref, tws_ref, wb_ref, wc_ref, y_ref, ys_ref):
    re, im = [], []
    for j in range(DFT_ROWS):
        tr, ti = tr_ref[j], ti_ref[j]
        cw = jnp.concatenate([twc_ref[j]] * FGROUPS, axis=1)
        sw = jnp.concatenate([tws_ref[j]] * FGROUPS, axis=1)
        re.append((tr * cw + ti * sw).astype(BF16))
        im.append((ti * cw - tr * sw).astype(BF16))
    st = jnp.concatenate([jnp.concatenate(re, axis=1), jnp.concatenate(im, axis=1)], axis=0)
    g = jnp.dot(wb_ref[...], st, preferred_element_type=F32).astype(BF16)
    n_blk = DFT_ROWS * FGROUPS
    gg = jnp.concatenate(
        [jnp.concatenate([g[:DFT_N, b * FDIM:(b + 1) * FDIM], g[DFT_N:, b * FDIM:(b + 1) * FDIM]], axis=1)
         for b in range(n_blk)], axis=0)
    y = jnp.dot(gg, wc_ref[...], preferred_element_type=F32)
    for j in range(DFT_ROWS):
        for grp in range(FGROUPS):
            b = j * FGROUPS + grp
            ys_ref[j, :, grp * FDIM:(grp + 1) * FDIM] = y[b * DFT_N:(b + 1) * DFT_N]
    for j in range(DFT_ROWS):
        y_ref[:, j, :] = ys_ref[j]


def _fourier_mix(f, twc, tws):
    S = f.shape[0]
    w_a, w_b, w_c = _dft_consts()
    cube = (DFT_N, DFT_N, FWIDTH)
    steps = (DFT_N // DFT_ROWS,)
    inner = pl.BlockSpec((DFT_N, DFT_ROWS, FWIDTH), lambda i: (0, i, 0))
    outer = pl.BlockSpec((DFT_ROWS, DFT_N, FWIDTH), lambda i: (i, 0, 0))
    tr, ti = pl.pallas_call(
        _dft_a_kernel,
        grid=steps,
        in_specs=[inner, _resident(w_a.shape)],
        out_specs=[inner, inner],
        out_shape=[jax.ShapeDtypeStruct(cube, F32)] * 2,
        scratch_shapes=[pltpu.VMEM((DFT_ROWS, DFT_N, FWIDTH), F32),
                        pltpu.VMEM((DFT_ROWS, 2 * DFT_N, FWIDTH), F32)],
        compiler_params=_params(1),
        name="dft_positions_outer",
    )(f.reshape(cube), w_a)
    y = pl.pallas_call(
        _dft_b_kernel,
        grid=steps,
        in_specs=[outer, outer,
                  pl.BlockSpec((DFT_ROWS, DFT_N, FDIM), lambda i: (i, 0, 0)),
                  pl.BlockSpec((DFT_ROWS, DFT_N, FDIM), lambda i: (i, 0, 0)),
                  _resident(w_b.shape), _resident(w_c.shape)],
        out_specs=inner,
        out_shape=jax.ShapeDtypeStruct(cube, F32),
        scratch_shapes=[pltpu.VMEM((DFT_ROWS, DFT_N, FWIDTH), F32)],
        compiler_params=_params(1),
        name="dft_positions_inner_channels",
    )(tr, ti, twc, tws, w_b, w_c)
    return y.reshape(S, FWIDTH)


def _even_out_ffn_kernel(a_ref, f_ref, x_ref, wa_ref, wf_ref, g_ref, b_ref,
                         wg_ref, wu_ref, wd_ref, fg_ref, fb_ref, o_ref):
    y = jnp.dot(a_ref[...], wa_ref[...], preferred_element_type=F32)
    y = y + jnp.dot(f_ref[...].astype(BF16), wf_ref[...], preferred_element_type=F32)
    h = _layer_norm(DN_ALPHA * x_ref[...] + y, g_ref[...], b_ref[...])
    o_ref[...] = _swiglu_ln(h, wg_ref, wu_ref, wd_ref, fg_ref, fb_ref)


def _even_out_ffn(attn, four, x, wo, g, b, wg, wu, wd, fg, fb, i, layer):
    S = x.shape[0]
    tm = ROW_TILE
    row = lambda r: (r, 0)
    n_attn = HEADS * V_DIM
    wa = pl.BlockSpec((None, n_attn, D_MODEL), lambda *_: (i, 0, 0), pipeline_mode=pl.Buffered(1))
    wf = pl.BlockSpec((None, FWIDTH, D_MODEL), lambda *_: (i, n_attn // FWIDTH, 0),
                      pipeline_mode=pl.Buffered(1))
    return pl.pallas_call(
        _even_out_ffn_kernel,
        grid=(S // tm,),
        in_specs=[pl.BlockSpec((tm, n_attn), row), pl.BlockSpec((tm, FWIDTH), row),
                  pl.BlockSpec((tm, D_MODEL), row),
                  wa, wf, _layer_of(g.shape, layer), _layer_of(b.shape, layer),
                  _layer_of(wg.shape, layer), _layer_of(wu.shape, layer), _layer_of(wd.shape, layer),
                  _layer_of(fg.shape, layer), _layer_of(fb.shape, layer)],
        out_specs=pl.BlockSpec((tm, D_MODEL), row),
        out_shape=jax.ShapeDtypeStruct((S, D_MODEL), F32),
        compiler_params=_params(1),
        name="even_out_ln_swiglu_ln",
    )(attn, four, x, wo, wo, g, b, wg, wu, wd, fg, fb)


def _odd_kernel(x_ref, win_ref, ng_ref, nb_ref, ws_ref, bs_ref, wout_ref, g_ref, b_ref,
                wg_ref, wu_ref, wd_ref, fg_ref, fb_ref, o_ref):
    tm = x_ref.shape[0]
    x = x_ref[...]
    z = jnp.dot(x.astype(BF16), win_ref[...], preferred_element_type=F32)
    z = 0.5 * z * (1.0 + lax.erf(z * (2.0 ** -0.5)))
    u = z[:, :SGU_WIDTH]
    v = _layer_norm(z[:, SGU_WIDTH:], ng_ref[...], nb_ref[...]).astype(BF16)
    gated = []
    for c in range(tm // SGU_CHUNK):
        rows = slice(c * SGU_CHUNK, (c + 1) * SGU_CHUNK)
        parts = []
        for grp in range(SGU_GROUPS):
            cols = slice(grp * SGU_GDIM, (grp + 1) * SGU_GDIM)
            s = jnp.dot(ws_ref[grp], v[rows, cols], preferred_element_type=F32) + bs_ref[grp]
            parts.append((u[rows, cols] * s).astype(BF16))
        gated.append(jnp.concatenate(parts, axis=1))
    gated = jnp.concatenate(gated, axis=0)
    y = jnp.dot(gated, wout_ref[...], preferred_element_type=F32)
    h = _layer_norm(DN_ALPHA * x + y, g_ref[...], b_ref[...])
    o_ref[...] = _swiglu_ln(h, wg_ref, wu_ref, wd_ref, fg_ref, fb_ref)


def _odd_layer(x, win, ng, nb, ws, bs, wout, g, b, wg, wu, wd, fg, fb, i, layer):
    S = x.shape[0]
    tm = SGU_ROWS
    row = lambda r: (r, 0)
    return pl.pallas_call(
        _odd_kernel,
        grid=(S // tm,),
        in_specs=[pl.BlockSpec((tm, D_MODEL), row),
                  _layer_of(win.shape, i), _layer_of(ng.shape, i), _layer_of(nb.shape, i),
                  _layer_of(ws.shape, i), _layer_of(bs.shape, i), _layer_of(wout.shape, i),
                  _layer_of(g.shape, layer), _layer_of(b.shape, layer),
                  _layer_of(wg.shape, layer), _layer_of(wu.shape, layer), _layer_of(wd.shape, layer),
                  _layer_of(fg.shape, layer), _layer_of(fb.shape, layer)],
        out_specs=pl.BlockSpec((tm, D_MODEL), row),
        out_shape=jax.ShapeDtypeStruct((S, D_MODEL), F32),
        compiler_params=_params(1),
        name="odd_sgu_ln_swiglu_ln",
    )(x, win, ng, nb, ws, bs, wout, g, b, wg, wu, wd, fg, fb)


def _swiglu_ln(x, wg_ref, wu_ref, wd_ref, g_ref, b_ref):
    xb = x.astype(BF16)
    y = None
    start = 0
    for width in FF_CHUNKS:
        cols = slice(start, start + width)
        start += width
        gate = jnp.dot(xb, wg_ref[:, cols], preferred_element_type=F32)
        up = jnp.dot(xb, wu_ref[:, cols], preferred_element_type=F32)
        hid = (jax.nn.silu(gate) * up).astype(BF16)
        part = jnp.dot(hid, wd_ref[cols, :], preferred_element_type=F32)
        y = part if y is None else y + part
    return _layer_norm(DN_ALPHA * x + y, g_ref[...], b_ref[...])


def _rotary_tables(seq):
    inv = 1.0 / (ROPE_THETA ** (jnp.arange(0, ROPE, 2, dtype=F32) / ROPE))
    ang = jnp.arange(seq, dtype=F32)[:, None] * inv[None, :]
    cos, sin = lax.optimization_barrier((jnp.cos(ang), jnp.sin(ang)))
    eye = np.eye(ROPE // 2, dtype=np.float32)
    rep_c = jnp.asarray(np.concatenate([eye, eye, eye, eye], axis=1))
    rep_s = jnp.asarray(np.concatenate([-eye, eye, -eye, eye], axis=1))
    cc = jnp.dot(cos, rep_c, precision=lax.Precision.HIGHEST)
    ss = jnp.dot(sin, rep_s, precision=lax.Precision.HIGHEST)
    return cos.T, sin.T, cc, ss


def _twiddle_tables():
    k2 = jnp.arange(DFT_N, dtype=jnp.int32)[:, None]
    n1 = jnp.arange(DFT_N, dtype=jnp.int32)[None, :]
    ang = (k2 * n1).astype(F32) * F32(2.0 * np.pi / SEQ)
    ang = ang.reshape(DFT_N, DFT_N, 1)
    cos, sin = lax.optimization_barrier((jnp.cos(ang), jnp.sin(ang)))
    return (jnp.broadcast_to(cos, (DFT_N, DFT_N, FDIM)),
            jnp.broadcast_to(sin, (DFT_N, DFT_N, FDIM)))


def kernel(x, even_w_in, even_q_norm, even_w_uq, even_kv_norm, even_w_uk, even_w_uv, even_w_out,
           odd_w_in, odd_sgu_norm_g, odd_sgu_norm_b, odd_w_spatial, odd_b_spatial, odd_w_out,
           mix_ln_g, mix_ln_b, ffn_w_gate, ffn_w_up, ffn_w_down, ffn_ln_g, ffn_ln_b):
    B, S, D = x.shape
    assert (B, S, D) == (1, SEQ, D_MODEL)
    cosT, sinT, cc, ss = _rotary_tables(S)
    twc, tws = _twiddle_tables()
    half = ROPE // 2
    c0 = Q_RANK + KV_RANK
    c1 = c0 + ROPE
    w = even_w_in
    even_win = jnp.concatenate(
        [w[..., :c0], w[..., c1:], w[..., c0:c1], w[..., c0 + half:c1], w[..., c0:c0 + half]],
        axis=-1).astype(BF16)
    even_qn = even_q_norm[:, None, :]
    even_kvn = even_kv_norm[:, None, :]
    even_wuqT = jnp.swapaxes(even_w_uq, 1, 2).astype(BF16)
    even_wuk = even_w_uk.astype(BF16)
    even_wuvT = jnp.swapaxes(even_w_uv, 1, 2).astype(BF16)
    even_wo = even_w_out.astype(BF16)
    odd_win = odd_w_in.astype(BF16)
    odd_ng = odd_sgu_norm_g[:, None, :]
    odd_nb = odd_sgu_norm_b[:, None, :]
    odd_ws = odd_w_spatial.astype(BF16)
    odd_bs = odd_b_spatial[..., None]
    odd_wout = odd_w_out.astype(BF16)
    mix_g, mix_b = mix_ln_g[:, None, :], mix_ln_b[:, None, :]
    ffn_g, ffn_b = ffn_ln_g[:, None, :], ffn_ln_b[:, None, :]
    ffn_wg, ffn_wu, ffn_wd = ffn_w_gate.astype(BF16), ffn_w_up.astype(BF16), ffn_w_down.astype(BF16)

    h = x.reshape(S, D)
    for layer in range(DEPTH):
        i = layer // 2
        if layer % 2 == 0:
            qT, k, vT, f, qmax_sq, kmax_sq, vmax = _even_in(
                h, even_win, even_qn, even_wuqT, even_kvn, even_wuk, even_wuvT, cosT, sinT, cc, ss, i)
            attn = _attention(_score_bound_flags(qmax_sq, kmax_sq, vmax), qT, k, vT)
            four = _fourier_mix(f, twc, tws)
            h = _even_out_ffn(attn, four, h, even_wo, mix_g, mix_b,
                              ffn_wg, ffn_wu, ffn_wd, ffn_g, ffn_b, i, layer)
        else:
            h = _odd_layer(h, odd_win, odd_ng, odd_nb, odd_ws, odd_bs, odd_wout, mix_g, mix_b,
                           ffn_wg, ffn_wu, ffn_wd, ffn_g, ffn_b, i, layer)
    return h.reshape(B, S, D)
```
